```python
import math, functools
import jax, jax.numpy as jnp
from jax import lax
import numpy as np

D_MODEL = 1024
BATCH = 16
SEQ = 256
DEPTH = 4
DEC_BATCH = 8
DEC_SEQ = 2048
PAST_LEN = 512

GRID_W = 64
Q_BLOCK = 128
EPS = 1e-6
ROPE_THETA = 10000.0
N_EVEN = (DEPTH + 1) // 2
N_ODD = DEPTH // 2
N_MOD = 6
HY_DIM = D_MODEL // 2
HY_ORDER = 2
FILT_EMB = 33
FILT_BANDS = (FILT_EMB - 1) // 2
FILT_HIDDEN = 64
HY_MIN_DECAY = math.log(1e-2) / 1.5
HY_MAX_DECAY = math.log(1e-2) / 0.3
GQA_HEADS = 8
GQA_KV_HEADS = 2
GQA_HEAD_DIM = 64
GQA_Q = GQA_HEADS * GQA_HEAD_DIM
GQA_KV = GQA_KV_HEADS * GQA_HEAD_DIM
EVEN_IN = (HY_ORDER + 1) * HY_DIM + GQA_Q + 2 * GQA_KV
EVEN_OUT = HY_DIM + GQA_Q
MLA_HEADS = 16
MLA_NOPE_DIM = 64
MLA_ROPE_DIM = 32
MLA_QK_DIM = MLA_NOPE_DIM + MLA_ROPE_DIM
MLA_V_DIM = 64
MLA_Q_RANK = 256
MLA_KV_RANK = 128
ODD_IN = MLA_Q_RANK + MLA_KV_RANK + MLA_ROPE_DIM
ODD_OUT = MLA_HEADS * MLA_V_DIM
N_EXPERTS = 16
N_GROUPS = 4
EXPERTS_PER_GROUP = N_EXPERTS // N_GROUPS
TOP_K = 2
D_EXPERT = 256

kernel_name = 'hybrid_diffusion_hyena_gqa_mla_moe_step'


def _rmsnorm(x, g):
    xf = x.astype(jnp.float32)
    y = xf * lax.rsqrt(jnp.mean(xf * xf, axis=-1, keepdims=True) + EPS)
    return (y * g.astype(jnp.float32)).astype(x.dtype)


def _grid_positions(n):
    rows = n // GRID_W
    row = jnp.repeat(jnp.arange(rows, dtype=jnp.int32), GRID_W)
    col = jnp.tile(jnp.arange(GRID_W, dtype=jnp.int32), rows)
    return row, col


def _rope_axis(x, pos):
    npair = x.shape[-1] // 2
    freqs = ROPE_THETA ** (-jnp.arange(npair, dtype=jnp.float32) / npair)
    ang = pos.astype(jnp.float32)[:, None] * freqs[None, :]
    cos = jnp.cos(ang)[None, :, None, :]
    sin = jnp.sin(ang)[None, :, None, :]
    xf = x.astype(jnp.float32)
    x1, x2 = xf[..., :npair], xf[..., npair:]
    return jnp.concatenate([x1 * cos - x2 * sin, x2 * cos + x1 * sin], axis=-1).astype(x.dtype)


def _rope_2d(x):
    row, col = _grid_positions(x.shape[1])
    half = x.shape[-1] // 2
    return jnp.concatenate([_rope_axis(x[..., :half], row), _rope_axis(x[..., half:], col)], axis=-1)


def _rope_tail(x, n_pass):
    return jnp.concatenate([x[..., :n_pass], _rope_2d(x[..., n_pass:])], axis=-1)


def _attend(q, k, v):
    b, lq, h, dh = q.shape
    hkv = k.shape[2]
    grp = h // hkv
    nb = lq // Q_BLOCK
    qb = q.reshape(b, nb, Q_BLOCK, hkv, grp, dh).transpose(1, 0, 2, 3, 4, 5)
    kf = k.astype(jnp.float32)
    vf = v.astype(jnp.float32)
    scale = dh ** -0.5

    def one_block(qblk):
        s = jnp.einsum('bqkgd,bskd->bkgqs', qblk.astype(jnp.float32), kf) * scale
        p = jax.nn.softmax(s, axis=-1)
        return jnp.einsum('bkgqs,bskd->bqkgd', p, vf)

    o = lax.map(one_block, qb)
    o = o.transpose(1, 0, 2, 3, 4, 5).reshape(b, lq, h, v.shape[-1])
    return o.astype(q.dtype)


def _short_conv3(u, w, b):
    zero = jnp.zeros_like(u[:, :1])
    prev = jnp.concatenate([zero, u[:, :-1]], axis=1)
    nxt = jnp.concatenate([u[:, 1:], zero], axis=1)
    return w[0] * prev + w[1] * u + w[2] * nxt + b


def _hyena_spectrum(n, w1, b1, w2, b2, w3, b3, freq):
    f32 = jnp.float32
    t01 = jnp.linspace(0.0, 1.0, n, dtype=f32)
    w = (2.0 * math.pi / n) * jnp.arange(n, dtype=f32)
    bands = jnp.linspace(1e-4, FILT_BANDS - 1, FILT_BANDS, dtype=f32)
    z = jnp.concatenate([t01[:, None], jnp.cos(w[:, None] * bands[None, :]),
                         -jnp.sin(w[:, None] * bands[None, :])], axis=-1)
    fr = freq.astype(f32)
    h = jnp.sin(fr * (z @ w1.astype(f32) + b1.astype(f32)))
    h = jnp.sin(fr * (h @ w2.astype(f32) + b2.astype(f32)))
    h = (h @ w3.astype(f32) + b3.astype(f32)).reshape(n, HY_ORDER, 2, HY_DIM)
    deltas = jnp.abs(jnp.linspace(HY_MIN_DECAY, HY_MAX_DECAY, HY_DIM, dtype=f32))
    h = h * jnp.exp(-t01[:, None] * deltas[None, :])[:, None, None, :]
    h_fwd, h_bwd = h[:, :, 0], h[:, :, 1]
    filt2n = jnp.concatenate([h_fwd, jnp.zeros((1, HY_ORDER, HY_DIM), f32), h_bwd[:0:-1]], axis=0)
    filt2n = filt2n * lax.rsqrt(jnp.sum(filt2n * filt2n, axis=0, keepdims=True) + EPS)
    return jnp.fft.rfft(filt2n, axis=0)


def _hyena(u, conv_w, conv_b, spec, bias):
    n = u.shape[1]
    uc = _short_conv3(u, conv_w, conv_b)
    v, x1, x2 = jnp.split(uc, HY_ORDER + 1, axis=-1)

    def long_conv(z, spec_o, b_o):
        zf = z.astype(jnp.float32)
        y = jnp.fft.irfft(jnp.fft.rfft(zf, n=2 * n, axis=1) * spec_o[None], n=2 * n, axis=1)[:, :n]
        return y + b_o.astype(jnp.float32) * zf

    z = x1.astype(jnp.float32) * long_conv(v, spec[:, 0], bias[0])
    z = x2.astype(jnp.float32) * long_conv(z, spec[:, 1], bias[1])
    return z.astype(u.dtype)


def _moe(h, router_w, router_bias, w_gate, w_up, w_down):
    b, n, d = h.shape
    x = h.reshape(b * n, d)
    scores = jax.nn.sigmoid(jnp.dot(x.astype(jnp.float32), router_w.astype(jnp.float32)))
    sel = scores + router_bias.astype(jnp.float32)
    group_score = jnp.sum(lax.top_k(sel.reshape(-1, N_GROUPS, EXPERTS_PER_GROUP), TOP_K)[0], axis=-1)
    g_best = jnp.argmax(group_score, axis=-1)
    in_group = (jnp.arange(N_EXPERTS) // EXPERTS_PER_GROUP)[None, :] == g_best[:, None]
    _, idx = lax.top_k(jnp.where(in_group, sel, -jnp.inf), TOP_K)
    wsel = jnp.take_along_axis(scores, idx, axis=-1)
    wsel = wsel / jnp.sum(wsel, axis=-1, keepdims=True)
    gates = jnp.sum(jax.nn.one_hot(idx, N_EXPERTS, dtype=jnp.float32) * wsel[..., None], axis=1)
    a = jnp.einsum('td,edf->tef', x, w_gate)
    up = jnp.einsum('td,edf->tef', x, w_up)
    mid = jax.nn.silu(a) * up * gates[:, :, None].astype(x.dtype)
    y = jnp.einsum('tef,efd->td', mid, w_down)
    return y.reshape(b, n, d)


def setup_inputs(seed: int = 0) -> dict:
    key = jax.random.key(seed)
    ks = iter(jax.random.split(key, 64))

    def nrm(shape, scale=1.0):
        return jax.random.normal(next(ks), shape, jnp.float32) * scale

    def gain(shape):
        return 1.0 + nrm(shape, 0.02)

    return {
        'x_prompt': nrm((BATCH, SEQ, D_MODEL)),
        'x_sample': nrm((DEC_BATCH, DEC_SEQ, D_MODEL)),
        'cache_gqa_k': nrm((DEC_BATCH, N_EVEN, PAST_LEN, GQA_KV_HEADS, GQA_HEAD_DIM)),
        'cache_gqa_v': nrm((DEC_BATCH, N_EVEN, PAST_LEN, GQA_KV_HEADS, GQA_HEAD_DIM)),
        'cache_mla_ckv': nrm((DEC_BATCH, N_ODD, PAST_LEN, MLA_KV_RANK)),
        'cache_mla_krope': nrm((DEC_BATCH, N_ODD, PAST_LEN, MLA_ROPE_DIM)),
        'c': nrm((DEC_BATCH, D_MODEL)),
        'c_ctx': nrm((D_MODEL,)),
        'norm_mix': gain((DEPTH, D_MODEL)),
        'norm_ffn': gain((DEPTH, D_MODEL)),
        'w_mod': nrm((DEPTH, D_MODEL, N_MOD * D_MODEL), 0.5 * D_MODEL ** -0.5),
        'b_mod': nrm((DEPTH, N_MOD * D_MODEL), 0.02),
        'ev_w_in': nrm((N_EVEN, D_MODEL, EVEN_IN), D_MODEL ** -0.5),
        'ev_conv_w': nrm((N_EVEN, 3, (HY_ORDER + 1) * HY_DIM), 0.5),
        'ev_conv_b': nrm((N_EVEN, (HY_ORDER + 1) * HY_DIM), 0.02),
        'ev_filt_w1': nrm((N_EVEN, FILT_EMB, FILT_HIDDEN), FILT_EMB ** -0.5),
        'ev_filt_b1': nrm((N_EVEN, FILT_HIDDEN), 0.02),
        'ev_filt_w2': nrm((N_EVEN, FILT_HIDDEN, FILT_HIDDEN), FILT_HIDDEN ** -0.5),
        'ev_filt_b2': nrm((N_EVEN, FILT_HIDDEN), 0.02),
        'ev_filt_w3': nrm((N_EVEN, FILT_HIDDEN, HY_ORDER * 2 * HY_DIM), FILT_HIDDEN ** -0.5),
        'ev_filt_b3': nrm((N_EVEN, HY_ORDER * 2 * HY_DIM), 0.02),
        'ev_filt_freq': gain((N_EVEN, FILT_HIDDEN)),
        'ev_hy_bias': nrm((N_EVEN, HY_ORDER, HY_DIM), 0.5),
        'ev_q_norm': gain((N_EVEN, GQA_HEAD_DIM)),
        'ev_k_norm': gain((N_EVEN, GQA_HEAD_DIM)),
        'ev_w_out': nrm((N_EVEN, EVEN_OUT, D_MODEL), EVEN_OUT ** -0.5),
        'od_w_in': nrm((N_ODD, D_MODEL, ODD_IN), D_MODEL ** -0.5),
        'od_q_lora_norm': gain((N_ODD, MLA_Q_RANK)),
        'od_kv_lora_norm': gain((N_ODD, MLA_KV_RANK)),
        'od_w_uq': nrm((N_ODD, MLA_Q_RANK, MLA_HEADS * MLA_QK_DIM), MLA_Q_RANK ** -0.5),
        'od_w_ukv': nrm((N_ODD, MLA_KV_RANK, MLA_HEADS * (MLA_NOPE_DIM + MLA_V_DIM)), MLA_KV_RANK ** -0.5),
        'od_q_norm': gain((N_ODD, MLA_QK_DIM)),
        'od_k_norm': gain((N_ODD, MLA_QK_DIM)),
        'od_w_out': nrm((N_ODD, ODD_OUT, D_MODEL), ODD_OUT ** -0.5),
        'router_w': nrm((D_MODEL, N_EXPERTS), D_MODEL ** -0.5),
        'router_bias': nrm((N_EXPERTS,), 0.01),
        'moe_w_gate': nrm((DEPTH, N_EXPERTS, D_MODEL, D_EXPERT), D_MODEL ** -0.5),
        'moe_w_up': nrm((DEPTH, N_EXPERTS, D_MODEL, D_EXPERT), D_MODEL ** -0.5),
        'moe_w_down': nrm((DEPTH, N_EXPERTS, D_EXPERT, D_MODEL), D_EXPERT ** -0.5),
    }


def reference(x_prompt, x_sample, cache_gqa_k, cache_gqa_v, cache_mla_ckv, cache_mla_krope, c,
              c_ctx, norm_mix, norm_ffn, w_mod, b_mod,
              ev_w_in, ev_conv_w, ev_conv_b, ev_filt_w1, ev_filt_b1, ev_filt_w2, ev_filt_b2,
              ev_filt_w3, ev_filt_b3, ev_filt_freq, ev_hy_bias, ev_q_norm, ev_k_norm, ev_w_out,
              od_w_in, od_q_lora_norm, od_kv_lora_norm, od_w_uq, od_w_ukv, od_q_norm, od_k_norm, od_w_out,
              router_w, router_bias, moe_w_gate, moe_w_up, moe_w_down):

    def modulation(cond, l):
        return jax.nn.silu(cond) @ w_mod[l] + b_mod[l]

    def even_mixer(h, i, ctx_k=None, ctx_v=None):
        b, n, _ = h.shape
        u = h @ ev_w_in[i]
        s0 = (HY_ORDER + 1) * HY_DIM
        u_hy, q, k, v = jnp.split(u, [s0, s0 + GQA_Q, s0 + GQA_Q + GQA_KV], axis=-1)
        spec = _hyena_spectrum(n, ev_filt_w1[i], ev_filt_b1[i], ev_filt_w2[i], ev_filt_b2[i],
                               ev_filt_w3[i], ev_filt_b3[i], ev_filt_freq[i])
        y_hy = _hyena(u_hy, ev_conv_w[i], ev_conv_b[i], spec, ev_hy_bias[i])
        q = _rmsnorm(q.reshape(b, n, GQA_HEADS, GQA_HEAD_DIM), ev_q_norm[i])
        k = _rmsnorm(k.reshape(b, n, GQA_KV_HEADS, GQA_HEAD_DIM), ev_k_norm[i])
        v = v.reshape(b, n, GQA_KV_HEADS, GQA_HEAD_DIM)
        if ctx_k is None:
            o = _attend(q, k, v)
        else:
            q = _rope_2d(q)
            k = _rope_2d(k)
            o = _attend(q, jnp.concatenate([ctx_k, k], axis=1), jnp.concatenate([ctx_v, v], axis=1))
        y = jnp.concatenate([y_hy, o.reshape(b, n, GQA_Q)], axis=-1) @ ev_w_out[i]
        return y, (k, v)

    def mla_kv(c_kv, k_rope, i):
        b, n, _ = c_kv.shape
        kv = (c_kv @ od_w_ukv[i]).reshape(b, n, MLA_HEADS, MLA_NOPE_DIM + MLA_V_DIM)
        k_nope, v = jnp.split(kv, [MLA_NOPE_DIM], axis=-1)
        k_pe = jnp.broadcast_to(k_rope[:, :, None, :], (b, n, MLA_HEADS, MLA_ROPE_DIM))
        k = _rmsnorm(jnp.concatenate([k_nope, k_pe], axis=-1), od_k_norm[i])
        return k, v

    def odd_mixer(h, i, ctx_ckv=None, ctx_krope=None):
        b, n, _ = h.shape
        u = h @ od_w_in[i]
        c_q, c_kv, k_rope = jnp.split(u, [MLA_Q_RANK, MLA_Q_RANK + MLA_KV_RANK], axis=-1)
        c_q = _rmsnorm(c_q, od_q_lora_norm[i])
        c_kv = _rmsnorm(c_kv, od_kv_lora_norm[i])
        q = _rmsnorm((c_q @ od_w_uq[i]).reshape(b, n, MLA_HEADS, MLA_QK_DIM), od_q_norm[i])
        k, v = mla_kv(c_kv, k_rope, i)
        if ctx_ckv is not None:
            q = _rope_tail(q, MLA_NOPE_DIM)
            k = _rope_tail(k, MLA_NOPE_DIM)
            k_c, v_c = mla_kv(ctx_ckv, ctx_krope, i)
            k = jnp.concatenate([k_c, k], axis=1)
            v = jnp.concatenate([v_c, v], axis=1)
        o = _attend(q, k, v)
        return o.reshape(b, n, ODD_OUT) @ od_w_out[i], (c_kv, k_rope)

    def layer(x, l, mod, mixer):
        sh1, sc1, g1, sh2, sc2, g2 = jnp.split(mod, N_MOD, axis=-1)
        y, aux = mixer(_rmsnorm(x, norm_mix[l]) * (1.0 + sc1) + sh1)
        x = x + g1 * y
        h = _rmsnorm(x, norm_ffn[l]) * (1.0 + sc2) + sh2
        x = x + g2 * _moe(h, router_w, router_bias, moe_w_gate[l], moe_w_up[l], moe_w_down[l])
        return x, aux

    xc = x_prompt
    st_k, st_v, st_ckv, st_kr = [], [], [], []
    for l in range(DEPTH):
        i = l // 2
        mod = modulation(c_ctx, l)
        if l % 2 == 0:
            xc, (k_c, v_c) = layer(xc, l, mod, functools.partial(even_mixer, i=i))
            st_k.append(k_c)
            st_v.append(v_c)
        else:
            xc, (ckv_c, kr_c) = layer(xc, l, mod, functools.partial(odd_mixer, i=i))
            st_ckv.append(ckv_c)
            st_kr.append(kr_c)
    y_prompt = xc
    state_gqa_k = jnp.stack(st_k, axis=1)
    state_gqa_v = jnp.stack(st_v, axis=1)
    state_mla_ckv = jnp.stack(st_ckv, axis=1)
    state_mla_krope = jnp.stack(st_kr, axis=1)

    xs = x_sample
    for l in range(DEPTH):
        i = l // 2
        mod = modulation(c, l)[:, None, :]
        if l % 2 == 0:
            mixer = functools.partial(even_mixer, i=i, ctx_k=cache_gqa_k[:, i], ctx_v=cache_gqa_v[:, i])
        else:
            mixer = functools.partial(odd_mixer, i=i, ctx_ckv=cache_mla_ckv[:, i], ctx_krope=cache_mla_krope[:, i])
        xs, _ = layer(xs, l, mod, mixer)
    y_sample = xs

    return (y_prompt, y_sample, state_gqa_k, state_gqa_v, state_mla_ckv, state_mla_krope)
```

```python
import functools
import math

import jax
import jax.numpy as jnp
from jax import lax
from jax.experimental import pallas as pl
from jax.experimental.pallas import tpu as pltpu

F32 = jnp.float32
BF16 = jnp.bfloat16

D_MODEL = 1024
BATCH = 16
SEQ = 256
DEPTH = 4
DEC_BATCH = 8
DEC_SEQ = 2048
PAST_LEN = 512
GRID_W = 64
EPS = 1e-6
ROPE_THETA = 10000.0
N_MOD = 6
HY_DIM = D_MODEL // 2
HY_ORDER = 2
FILT_EMB = 33
FILT_BANDS = (FILT_EMB - 1) // 2
HY_MIN_DECAY = math.log(1e-2) / 1.5
HY_MAX_DECAY = math.log(1e-2) / 0.3
GQA_HEADS = 8
GQA_KV_HEADS = 2
GQA_HEAD_DIM = 64
GQA_GROUP = GQA_HEADS // GQA_KV_HEADS
GQA_Q = GQA_HEADS * GQA_HEAD_DIM
GQA_KV = GQA_KV_HEADS * GQA_HEAD_DIM
HY_IN = (HY_ORDER + 1) * HY_DIM
MLA_HEADS = 16
MLA_NOPE_DIM = 64
MLA_ROPE_DIM = 32
MLA_QK_DIM = MLA_NOPE_DIM + MLA_ROPE_DIM
MLA_V_DIM = 64
MLA_Q_RANK = 256
MLA_KV_RANK = 128
N_EXPERTS = 16
N_GROUPS = 4
EXPERTS_PER_GROUP = N_EXPERTS // N_GROUPS
D_EXPERT = 256

T_CTX = BATCH * SEQ
T_LAT = DEC_BATCH * DEC_SEQ
T_ALL = T_CTX + T_LAT
N_COND = 1 + DEC_BATCH

VMEM_LIMIT_BYTES = 48 * 1024 * 1024


def _params(*sem):
    return pltpu.CompilerParams(dimension_semantics=sem, vmem_limit_bytes=VMEM_LIMIT_BYTES)


def _cond_index(tm):
    n_ctx_tiles = T_CTX // tm
    tiles_per_seq = DEC_SEQ // tm
    return lambda i: (jnp.where(i < n_ctx_tiles, 0, 1 + (i - n_ctx_tiles) // tiles_per_seq), 0, 0)


def _norm_mod(x, gain, mod_ref, shift_row, scale_row):
    r = lax.rsqrt(jnp.mean(x * x, axis=-1, keepdims=True) + EPS)
    h = (x * r) * gain
    return h * (1.0 + mod_ref[scale_row:scale_row + 1, :]) + mod_ref[shift_row:shift_row + 1, :]


def _mod_kernel(c_ref, w_ref, b_ref, o_ref):
    c = c_ref[...]
    a = (c * jax.nn.sigmoid(c)).astype(BF16)
    o_ref[...] = jnp.dot(a, w_ref[...].astype(BF16), preferred_element_type=F32) + b_ref[...]


def _modulation(cond, w_mod, b_mod):
    rows = cond.shape[0]
    tn = 1536
    nj = (N_MOD * D_MODEL) // tn
    return pl.pallas_call(
        _mod_kernel,
        out_shape=jax.ShapeDtypeStruct((DEPTH, rows, N_MOD * D_MODEL), F32),
        grid=(DEPTH, nj),
        in_specs=[
            pl.BlockSpec((rows, D_MODEL), lambda l, j: (0, 0)),
            pl.BlockSpec((None, D_MODEL, tn), lambda l, j: (l, 0, j)),
            pl.BlockSpec((None, 1, tn), lambda l, j: (l, 0, j)),
        ],
        out_specs=pl.BlockSpec((None, rows, tn), lambda l, j: (l, 0, j)),
        compiler_params=_params("parallel", "parallel"),
        name="modulation",
    )(cond, w_mod, b_mod.reshape(DEPTH, 1, N_MOD * D_MODEL))


def _nmm_kernel(x_ref, g_ref, mod_ref, w_ref, o_ref, *, shift_row, scale_row):
    h = _norm_mod(x_ref[...], g_ref[...], mod_ref, shift_row, scale_row)
    o_ref[...] = jnp.dot(h.astype(BF16), w_ref[...], preferred_element_type=F32).astype(o_ref.dtype)


def _norm_mod_matmul(x, gain, mod, w, shift_row, scale_row, tm=512):
    n = w.shape[1]
    return pl.pallas_call(
        functools.partial(_nmm_kernel, shift_row=shift_row, scale_row=scale_row),
        out_shape=jax.ShapeDtypeStruct((T_ALL, n), F32),
        grid=(T_ALL // tm,),
        in_specs=[
            pl.BlockSpec((tm, D_MODEL), lambda i: (i, 0)),
            pl.BlockSpec((1, D_MODEL), lambda i: (0, 0)),
            pl.BlockSpec((None, N_MOD, D_MODEL), _cond_index(tm)),
            pl.BlockSpec((D_MODEL, n), lambda i: (0, 0)),
        ],
        out_specs=pl.BlockSpec((tm, n), lambda i: (i, 0)),
        compiler_params=_params("parallel"),
        name="norm_mod_matmul",
    )(x, gain.reshape(1, D_MODEL), mod, w)


def _mm_kernel(a_ref, w_ref, o_ref):
    o_ref[...] = jnp.dot(a_ref[...].astype(BF16), w_ref[...], preferred_element_type=F32).astype(o_ref.dtype)


def _matmul(a, w, tm=512):
    m, k = a.shape
    n = w.shape[1]
    return pl.pallas_call(
        _mm_kernel,
        out_shape=jax.ShapeDtypeStruct((m, n), F32),
        grid=(m // tm,),
        in_specs=[pl.BlockSpec((tm, k), lambda i: (i, 0)), pl.BlockSpec((k, n), lambda i: (0, 0))],
        out_specs=pl.BlockSpec((tm, n), lambda i: (i, 0)),
        compiler_params=_params("parallel"),
        name="matmul",
    )(a, w)


def _mmres_kernel(a_ref, w_ref, x_ref, mod_ref, o_ref, *, gate_row):
    y = jnp.dot(a_ref[...], w_ref[...], preferred_element_type=F32)
    o_ref[...] = x_ref[...] + mod_ref[gate_row:gate_row + 1, :] * y


def _matmul_gate_residual(a, w, x, mod, gate_row, tm=512):
    k = a.shape[1]
    return pl.pallas_call(
        functools.partial(_mmres_kernel, gate_row=gate_row),
        out_shape=jax.ShapeDtypeStruct((T_ALL, D_MODEL), F32),
        grid=(T_ALL // tm,),
        in_specs=[
            pl.BlockSpec((tm, k), lambda i: (i, 0)),
            pl.BlockSpec((k, D_MODEL), lambda i: (0, 0)),
            pl.BlockSpec((tm, D_MODEL), lambda i: (i, 0)),
            pl.BlockSpec((None, N_MOD, D_MODEL), _cond_index(tm)),
        ],
        out_specs=pl.BlockSpec((tm, D_MODEL), lambda i: (i, 0)),
        compiler_params=_params("parallel"),
        name="matmul_gate_residual",
    )(a, w, x, mod)


def _bmm_kernel(a_ref, x_ref, o_ref):
    o_ref[...] = jnp.dot(a_ref[...], x_ref[...], preferred_element_type=F32)


def _left_matmul(a, x, tm):
    m, k = a.shape
    b, _, n = x.shape
    tm = min(tm, m)
    return pl.pallas_call(
        _bmm_kernel,
        out_shape=jax.ShapeDtypeStruct((b, m, n), F32),
        grid=(b, m // tm),
        in_specs=[pl.BlockSpec((tm, k), lambda bi, i: (i, 0)), pl.BlockSpec((None, k, n), lambda bi, i: (bi, 0, 0))],
        out_specs=pl.BlockSpec((None, tm, n), lambda bi, i: (bi, i, 0)),
        compiler_params=_params("parallel", "parallel"),
        name="left_matmul",
    )(a, x)


def _attn_kernel(q_ref, k_ref, v_ref, o_ref, *, scale):
    g, tq, d = q_ref.shape
    q = q_ref[...].reshape(g * tq, d)
    s = lax.dot_general(q, k_ref[...], (((1,), (1,)), ((), ())), preferred_element_type=F32) * scale
    m = jnp.max(s, axis=-1, keepdims=True)
    p = jnp.exp(s - m)
    l = jnp.sum(p, axis=-1, keepdims=True)
    o = jnp.dot(p.astype(BF16), v_ref[...], preferred_element_type=F32) / l
    o_ref[...] = o.reshape(g, tq, o_ref.shape[-1]).astype(o_ref.dtype)


def _attention(q, k, v, tq):
    b, hkv, g, lq, d = q.shape
    lk = k.shape[2]
    dv = v.shape[-1]
    return pl.pallas_call(
        functools.partial(_attn_kernel, scale=d ** -0.5),
        out_shape=jax.ShapeDtypeStruct((b, hkv, g, lq, dv), BF16),
        grid=(b, hkv, lq // tq),
        in_specs=[
            pl.BlockSpec((None, None, g, tq, d), lambda bi, h, i: (bi, h, 0, i, 0)),
            pl.BlockSpec((None, None, lk, d), lambda bi, h, i: (bi, h, 0, 0)),
            pl.BlockSpec((None, None, lk, dv), lambda bi, h, i: (bi, h, 0, 0)),
        ],
        out_specs=pl.BlockSpec((None, None, g, tq, dv), lambda bi, h, i: (bi, h, 0, i, 0)),
        compiler_params=_params("parallel", "parallel", "parallel"),
        name="attention",
    )(q, k, v)


def _router_kernel(x_ref, g_ref, mod_ref, rw_ref, rb_ref, h_ref, gate_ref):
    h = _norm_mod(x_ref[...], g_ref[...], mod_ref, 3, 4)
    h_ref[...] = h.astype(BF16)
    logits = lax.dot_general(rw_ref[...], h, (((1,), (1,)), ((), ())),
                             precision=lax.Precision.HIGHEST, preferred_element_type=F32)
    score = jax.nn.sigmoid(logits)
    sel = score + rb_ref[...]
    ng = N_GROUPS
    sv = [sel[p * ng:(p + 1) * ng, :] for p in range(EXPERTS_PER_GROUP)]
    hi01, lo01 = jnp.maximum(sv[0], sv[1]), jnp.minimum(sv[0], sv[1])
    hi23, lo23 = jnp.maximum(sv[2], sv[3]), jnp.minimum(sv[2], sv[3])
    top1 = jnp.maximum(hi01, hi23)
    top2 = jnp.maximum(jnp.minimum(hi01, hi23), jnp.maximum(lo01, lo23))
    gscore = top1 + top2
    gidx = lax.broadcasted_iota(jnp.int32, gscore.shape, 0)
    beaten = jnp.zeros(gscore.shape, jnp.int32)
    for j in range(ng):
        other = gscore[j:j + 1, :]
        beaten += ((other > gscore) | ((other == gscore) & (j < gidx))).astype(jnp.int32)
    best = beaten == 0
    chosen = []
    for p in range(EXPERTS_PER_GROUP):
        rank = jnp.zeros(gscore.shape, jnp.int32)
        for pp in range(EXPERTS_PER_GROUP):
            if pp != p:
                wins = (sv[pp] > sv[p]) | ((sv[pp] == sv[p]) & (pp < p))
                rank += wins.astype(jnp.int32)
        chosen.append((rank < 2) & best)
    picked = [jnp.where(chosen[p], score[p * ng:(p + 1) * ng, :], 0.0) for p in range(EXPERTS_PER_GROUP)]
    total = jnp.sum(picked[0] + picked[1] + picked[2] + picked[3], axis=0, keepdims=True)
    gate_ref[...] = jnp.concatenate(picked, axis=0) / total


def _router(x, gain, mod, rw_pm, rb_pm, tm=512):
    return pl.pallas_call(
        _router_kernel,
        out_shape=(jax.ShapeDtypeStruct((T_ALL, D_MODEL), BF16), jax.ShapeDtypeStruct((N_EXPERTS, T_ALL), F32)),
        grid=(T_ALL // tm,),
        in_specs=[
            pl.BlockSpec((tm, D_MODEL), lambda i: (i, 0)),
            pl.BlockSpec((1, D_MODEL), lambda i: (0, 0)),
            pl.BlockSpec((None, N_MOD, D_MODEL), _cond_index(tm)),
            pl.BlockSpec((N_EXPERTS, D_MODEL), lambda i: (0, 0)),
            pl.BlockSpec((N_EXPERTS, 1), lambda i: (0, 0)),
        ],
        out_specs=(pl.BlockSpec((tm, D_MODEL), lambda i: (i, 0)), pl.BlockSpec((N_EXPERTS, tm), lambda i: (0, i))),
        compiler_params=_params("parallel"),
        name="router",
    )(x, gain.reshape(1, D_MODEL), mod, rw_pm, rb_pm)


def _moe_kernel(h_ref, gate_ref, wg_ref, wu_ref, wd_ref, x_ref, mod_ref, o_ref, acc_ref):
    e = pl.program_id(1)

    @pl.when(e == 0)
    def _():
        acc_ref[...] = jnp.zeros_like(acc_ref)

    h = h_ref[...]
    a = jnp.dot(h, wg_ref[...].astype(BF16), preferred_element_type=F32)
    u = jnp.dot(h, wu_ref[...].astype(BF16), preferred_element_type=F32)
    gates = gate_ref[...]
    lane = lax.broadcasted_iota(jnp.int32, gates.shape, 1)
    g = jnp.sum(jnp.where(lane == e, gates, 0.0), axis=1, keepdims=True)
    mid = (a * jax.nn.sigmoid(a)) * u * g
    acc_ref[...] += jnp.dot(mid.astype(BF16), wd_ref[...].astype(BF16), preferred_element_type=F32)

    @pl.when(e == N_EXPERTS - 1)
    def _():
        o_ref[...] = x_ref[...] + mod_ref[5:6, :] * acc_ref[...]


def _moe(h, gates, wg, wu, wd, x, mod, tm=1024):
    return pl.pallas_call(
        _moe_kernel,
        out_shape=jax.ShapeDtypeStruct((T_ALL, D_MODEL), F32),
        grid=(T_ALL // tm, N_EXPERTS),
        in_specs=[
            pl.BlockSpec((tm, D_MODEL), lambda i, e: (i, 0)),
            pl.BlockSpec((tm, N_EXPERTS), lambda i, e: (i, 0)),
            pl.BlockSpec((None, D_MODEL, D_EXPERT), lambda i, e: (e, 0, 0)),
            pl.BlockSpec((None, D_MODEL, D_EXPERT), lambda i, e: (e, 0, 0)),
            pl.BlockSpec((None, D_EXPERT, D_MODEL), lambda i, e: (e, 0, 0)),
            pl.BlockSpec((tm, D_MODEL), lambda i, e: (i, 0)),
            pl.BlockSpec((None, N_MOD, D_MODEL), lambda i, e: _cond_index(tm)(i)),
        ],
        out_specs=pl.BlockSpec((tm, D_MODEL), lambda i, e: (i, 0)),
        scratch_shapes=[pltpu.VMEM((tm, D_MODEL), F32)],
        compiler_params=_params("parallel", "arbitrary"),
        name="experts",
    )(h, gates, wg, wu, wd, x, mod)


def _head_norm(x, g):
    return x * lax.rsqrt(jnp.mean(x * x, axis=-1, keepdims=True) + EPS) * g


def _rope_axis(x, pos):
    npair = x.shape[-1] // 2
    freqs = ROPE_THETA ** (-jnp.arange(npair, dtype=F32) / npair)
    ang = pos.astype(F32)[:, None] * freqs[None, :]
    cos = jnp.cos(ang)[None, :, None, :]
    sin = jnp.sin(ang)[None, :, None, :]
    x1, x2 = x[..., :npair], x[..., npair:]
    return jnp.concatenate([x1 * cos - x2 * sin, x2 * cos + x1 * sin], axis=-1)


def _rope_2d(x):
    n = x.shape[1]
    t = jnp.arange(n, dtype=jnp.int32)
    half = x.shape[-1] // 2
    return jnp.concatenate([_rope_axis(x[..., :half], t // GRID_W), _rope_axis(x[..., half:], t % GRID_W)], axis=-1)


def _rope_tail(x, n_pass):
    return jnp.concatenate([x[..., :n_pass], _rope_2d(x[..., n_pass:])], axis=-1)


def _dft_matrices(n):
    f = jnp.arange(n, dtype=jnp.int32)[:, None]
    t = jnp.arange(n, dtype=jnp.int32)[None, :]
    ang = ((f * t) % (2 * n)).astype(F32) * (math.pi / n)
    alt = jnp.where(t % 2 == 0, 1.0, -1.0).astype(F32)
    fwd = jnp.concatenate([jnp.cos(ang), alt, -jnp.sin(ang)[1:]], axis=0)
    row = jnp.arange(2 * n, dtype=jnp.int32)[:, None]
    weight = jnp.where((row == 0) | (row == n), 0.5 / n, 1.0 / n).astype(F32)
    inv = (fwd * weight).T
    return fwd.astype(BF16), inv.astype(BF16)


def _spectral_product(z, h, n):
    zr, zi = z[..., :n, :], z[..., n:, :]
    hr, hi = h[:n], h[n:]
    first = (jnp.arange(n) == 0)[:, None]
    ii = zi * hi
    yr = zr * hr - jnp.where(first, 0.0, ii)
    yi = jnp.where(first, ii, zr * hi + zi * hr)
    return jnp.concatenate([yr, yi], axis=-2)


def _filter_spectrum(n, fwd, w1, b1, w2, b2, w3, b3, freq):
    hp = lax.Precision.HIGHEST
    t01 = jnp.linspace(0.0, 1.0, n, dtype=F32)
    w = (2.0 * math.pi / n) * jnp.arange(n, dtype=F32)
    bands = jnp.linspace(1e-4, FILT_BANDS - 1, FILT_BANDS, dtype=F32)
    z = jnp.concatenate([t01[:, None], jnp.cos(w[:, None] * bands[None, :]),
                         -jnp.sin(w[:, None] * bands[None, :])], axis=-1)
    h = jnp.sin(freq * (jnp.dot(z, w1, precision=hp) + b1))
    h = jnp.sin(freq * (jnp.dot(h, w2, precision=hp) + b2))
    h = (jnp.dot(h, w3, precision=hp) + b3).reshape(n, HY_ORDER, 2, HY_DIM)
    deltas = jnp.abs(jnp.linspace(HY_MIN_DECAY, HY_MAX_DECAY, HY_DIM, dtype=F32))
    h = h * jnp.exp(-t01[:, None] * deltas[None, :])[:, None, None, :]
    h_fwd = h[:, :, 0]
    h_bwd = h[:, :, 1] * (jnp.arange(n) > 0)[:, None, None].astype(F32)
    scale = lax.rsqrt(jnp.sum(h_fwd * h_fwd, axis=0, keepdims=True) + jnp.sum(h_bwd * h_bwd, axis=0, keepdims=True) + EPS)
    taps = jnp.concatenate([h_fwd * scale, h_bwd * scale], axis=1).reshape(1, n, 2 * HY_ORDER * HY_DIM)
    spec = _left_matmul(fwd, taps.astype(BF16), tm=1024)[0].reshape(2 * n, 2, HY_ORDER, HY_DIM)
    a, b = spec[:, 0], spec[:, 1]
    real_rows = (jnp.arange(2 * n) <= n)[:, None, None]
    return jnp.where(real_rows, a + b, a - b)


def _short_conv3(u, w, b):
    zero = jnp.zeros_like(u[:, :1])
    prev = jnp.concatenate([zero, u[:, :-1]], axis=1)
    nxt = jnp.concatenate([u[:, 1:], zero], axis=1)
    return w[0] * prev + w[1] * u + w[2] * nxt + b


def _hyena(u, conv_w, conv_b, spec, bias, fwd, inv):
    n = u.shape[1]
    uc = _short_conv3(u, conv_w, conv_b)
    v, x1, x2 = jnp.split(uc, HY_ORDER + 1, axis=-1)

    def long_conv(z, h, b_o):
        zf = _left_matmul(fwd, z.astype(BF16), tm=1024)
        y = _left_matmul(inv, _spectral_product(zf, h, n).astype(BF16), tm=1024)
        return y + b_o * z

    z = x1 * long_conv(v, spec[:, 0], bias[0])
    return x2 * long_conv(z, spec[:, 1], bias[1])


def _split_groups(a):
    return a[:T_CTX].reshape(BATCH, SEQ, -1), a[T_CTX:].reshape(DEC_BATCH, DEC_SEQ, -1)


def _merge_groups(a_ctx, a_lat):
    return jnp.concatenate([a_ctx.reshape(T_CTX, -1), a_lat.reshape(T_LAT, -1)], axis=0)


def _gqa_heads(q):
    b, n = q.shape[:2]
    return q.reshape(b, n, GQA_KV_HEADS, GQA_GROUP, GQA_HEAD_DIM).transpose(0, 2, 3, 1, 4).astype(BF16)


def _even_group(u, n, dft, filt, conv_w, conv_b, hy_bias, q_norm, k_norm, ctx_k=None, ctx_v=None):
    b = u.shape[0]
    u_hy, q, k, v = jnp.split(u, [HY_IN, HY_IN + GQA_Q, HY_IN + GQA_Q + GQA_KV], axis=-1)
    y_hy = _hyena(u_hy, conv_w, conv_b, filt, hy_bias, *dft)
    q = _head_norm(q.reshape(b, n, GQA_HEADS, GQA_HEAD_DIM), q_norm)
    k = _head_norm(k.reshape(b, n, GQA_KV_HEADS, GQA_HEAD_DIM), k_norm)
    v = v.reshape(b, n, GQA_KV_HEADS, GQA_HEAD_DIM)
    state = (k, v)
    if ctx_k is not None:
        q = _rope_2d(q)
        k = jnp.concatenate([ctx_k, _rope_2d(k)], axis=1)
        v = jnp.concatenate([ctx_v, v], axis=1)
    o = _attention(_gqa_heads(q), k.transpose(0, 2, 1, 3).astype(BF16), v.transpose(0, 2, 1, 3).astype(BF16),
                   tq=min(n, 128))
    o = o.transpose(0, 3, 1, 2, 4).reshape(b, n, GQA_Q)
    return jnp.concatenate([y_hy.astype(BF16), o], axis=-1), state


def kernel(x_prompt, x_sample, cache_gqa_k, cache_gqa_v, cache_mla_ckv, cache_mla_krope, c, c_ctx, norm_mix, norm_ffn, w_mod, b_mod, ev_w_in, ev_conv_w, ev_conv_b, ev_filt_w1, ev_filt_b1, ev_filt_w2, ev_filt_b2, ev_filt_w3, ev_filt_b3, ev_filt_freq, ev_hy_bias, ev_q_norm, ev_k_norm, ev_w_out, od_w_in, od_q_lora_norm, od_kv_lora_norm, od_w_uq, od_w_ukv, od_q_norm, od_k_norm, od_w_out, router_w, router_bias, moe_w_gate, moe_w_up, moe_w_down):
    x = jnp.concatenate([x_prompt.reshape(T_CTX, D_MODEL), x_sample.reshape(T_LAT, D_MODEL)], axis=0)
    cond = jnp.concatenate([c_ctx[None, :], c], axis=0)
    cond = jnp.pad(cond, ((0, 16 - N_COND), (0, 0)))
    mod_all = _modulation(cond, w_mod, b_mod).reshape(DEPTH, 16, N_MOD, D_MODEL)

    perm = jnp.arange(N_EXPERTS).reshape(N_GROUPS, EXPERTS_PER_GROUP).T.reshape(-1)
    rw_pm = router_w.T[perm]
    rb_pm = router_bias[perm].reshape(N_EXPERTS, 1)

    dft_ctx = _dft_matrices(SEQ)
    dft_lat = _dft_matrices(DEC_SEQ)

    st_k, st_v, st_ckv, st_kr = [], [], [], []
    for l in range(DEPTH):
        i = l // 2
        mod = mod_all[l]
        if l % 2 == 0:
            u = _norm_mod_matmul(x, norm_mix[l], mod, ev_w_in[i].astype(BF16), 0, 1)
            u_ctx, u_lat = _split_groups(u)
            filt_args = (ev_filt_w1[i], ev_filt_b1[i], ev_filt_w2[i], ev_filt_b2[i], ev_filt_w3[i], ev_filt_b3[i],
                         ev_filt_freq[i])
            common = (ev_conv_w[i], ev_conv_b[i], ev_hy_bias[i], ev_q_norm[i], ev_k_norm[i])
            a_ctx, (k_c, v_c) = _even_group(u_ctx, SEQ, dft_ctx, _filter_spectrum(SEQ, dft_ctx[0], *filt_args), *common)
            a_lat, _ = _even_group(u_lat, DEC_SEQ, dft_lat, _filter_spectrum(DEC_SEQ, dft_lat[0], *filt_args), *common,
                                   ctx_k=cache_gqa_k[:, i], ctx_v=cache_gqa_v[:, i])
            st_k.append(k_c)
            st_v.append(v_c)
            x = _matmul_gate_residual(_merge_groups(a_ctx, a_lat), ev_w_out[i].astype(BF16), x, mod, 2)
        else:
            u = _norm_mod_matmul(x, norm_mix[l], mod, od_w_in[i].astype(BF16), 0, 1)
            c_q, c_kv, k_rope = jnp.split(u, [MLA_Q_RANK, MLA_Q_RANK + MLA_KV_RANK], axis=-1)
            c_q = _head_norm(c_q, od_q_lora_norm[i])
            c_kv = _head_norm(c_kv, od_kv_lora_norm[i])
            ckv_ctx, ckv_lat = _split_groups(c_kv)
            kr_ctx, kr_lat = _split_groups(k_rope)
            st_ckv.append(ckv_ctx)
            st_kr.append(kr_ctx)
            q = _matmul(c_q, od_w_uq[i].astype(BF16))
            q = _head_norm(q.reshape(T_ALL, MLA_HEADS, MLA_QK_DIM), od_q_norm[i])
            ckv_rows = jnp.concatenate([c_kv, cache_mla_ckv[:, i].reshape(DEC_BATCH * PAST_LEN, MLA_KV_RANK)], axis=0)
            kv = _matmul(ckv_rows, od_w_ukv[i].astype(BF16)).reshape(-1, MLA_HEADS, MLA_NOPE_DIM + MLA_V_DIM)
            k_nope, v = kv[..., :MLA_NOPE_DIM], kv[..., MLA_NOPE_DIM:]
            kr_rows = jnp.concatenate([k_rope, cache_mla_krope[:, i].reshape(DEC_BATCH * PAST_LEN, MLA_ROPE_DIM)], axis=0)
            k_pe = jnp.broadcast_to(kr_rows[:, None, :], (kr_rows.shape[0], MLA_HEADS, MLA_ROPE_DIM))
            k = _head_norm(jnp.concatenate([k_nope, k_pe], axis=-1), od_k_norm[i])

            def heads(a, b, n):
                return a.reshape(b, n, MLA_HEADS, 1, a.shape[-1]).transpose(0, 2, 3, 1, 4).astype(BF16)

            q_ctx = heads(q[:T_CTX], BATCH, SEQ)
            k_ctx = heads(k[:T_CTX], BATCH, SEQ)[:, :, 0]
            v_ctx = heads(v[:T_CTX], BATCH, SEQ)[:, :, 0]
            o_ctx = _attention(q_ctx, k_ctx, v_ctx, tq=SEQ)

            q_l = _rope_tail(q[T_CTX:].reshape(DEC_BATCH, DEC_SEQ, MLA_HEADS, MLA_QK_DIM), MLA_NOPE_DIM)
            k_new = _rope_tail(k[T_CTX:T_ALL].reshape(DEC_BATCH, DEC_SEQ, MLA_HEADS, MLA_QK_DIM), MLA_NOPE_DIM)
            k_l = jnp.concatenate([k[T_ALL:].reshape(DEC_BATCH, PAST_LEN, MLA_HEADS, MLA_QK_DIM), k_new], axis=1)
            v_l = jnp.concatenate([v[T_ALL:].reshape(DEC_BATCH, PAST_LEN, MLA_HEADS, MLA_V_DIM),
                                   v[T_CTX:T_ALL].reshape(DEC_BATCH, DEC_SEQ, MLA_HEADS, MLA_V_DIM)], axis=1)
            o_lat = _attention(q_l.transpose(0, 2, 1, 3)[:, :, None].astype(BF16),
                               k_l.transpose(0, 2, 1, 3).astype(BF16), v_l.transpose(0, 2, 1, 3).astype(BF16), tq=512)

            def unheads(o, b, n):
                return o[:, :, 0].transpose(0, 2, 1, 3).reshape(b * n, MLA_HEADS * MLA_V_DIM)

            a = jnp.concatenate([unheads(o_ctx, BATCH, SEQ), unheads(o_lat, DEC_BATCH, DEC_SEQ)], axis=0)
            x = _matmul_gate_residual(a, od_w_out[i].astype(BF16), x, mod, 2)

        h, gates_pm = _router(x, norm_ffn[l], mod, rw_pm, rb_pm)
        gates = gates_pm.reshape(EXPERTS_PER_GROUP, N_GROUPS, T_ALL).transpose(2, 1, 0).reshape(T_ALL, N_EXPERTS)
        x = _moe(h, gates, moe_w_gate[l], moe_w_up[l], moe_w_down[l], x, mod)

    y_prompt = x[:T_CTX].reshape(BATCH, SEQ, D_MODEL)
    y_sample = x[T_CTX:].reshape(DEC_BATCH, DEC_SEQ, D_MODEL)
    return (y_prompt, y_sample, jnp.stack(st_k, axis=1), jnp.stack(st_v, axis=1),
            jnp.stack(st_ckv, axis=1), jnp.stack(st_kr, axis=1))
```

```python
import functools
import math

import jax
import jax.numpy as jnp
from jax import lax
from jax.experimental import pallas as pl
from jax.experimental.pallas import tpu as pltpu

F32 = jnp.float32
BF16 = jnp.bfloat16

D_MODEL = 1024
BATCH = 16
SEQ = 256
DEPTH = 4
DEC_BATCH = 8
DEC_SEQ = 2048
PAST_LEN = 512
GRID_W = 64
EPS = 1e-6
ROPE_THETA = 10000.0
N_MOD = 6
HY_DIM = D_MODEL // 2
HY_ORDER = 2
FILT_EMB = 33
FILT_BANDS = (FILT_EMB - 1) // 2
HY_MIN_DECAY = math.log(1e-2) / 1.5
HY_MAX_DECAY = math.log(1e-2) / 0.3
GQA_HEADS = 8
GQA_KV_HEADS = 2
GQA_HEAD_DIM = 64
GQA_GROUP = GQA_HEADS // GQA_KV_HEADS
GQA_Q = GQA_HEADS * GQA_HEAD_DIM
GQA_KV = GQA_KV_HEADS * GQA_HEAD_DIM
HY_IN = (HY_ORDER + 1) * HY_DIM
MLA_HEADS = 16
MLA_NOPE_DIM = 64
MLA_ROPE_DIM = 32
MLA_QK_DIM = MLA_NOPE_DIM + MLA_ROPE_DIM
MLA_V_DIM = 64
MLA_Q_RANK = 256
MLA_KV_RANK = 128
N_EXPERTS = 16
N_GROUPS = 4
EXPERTS_PER_GROUP = N_EXPERTS // N_GROUPS
D_EXPERT = 256

LANES = 128
T_CTX = BATCH * SEQ
T_LAT = DEC_BATCH * DEC_SEQ
T_ALL = T_CTX + T_LAT
N_COND = 1 + DEC_BATCH
COND_ROWS = 16
TM = 512
N_CTX_TILES = T_CTX // TM
TILES_PER_SEQ = DEC_SEQ // TM
GQA_SLOTS = GQA_HEADS * LANES
MLA_SLOTS = MLA_HEADS * LANES
EVEN_COLS = HY_IN + GQA_SLOTS + 2 * LANES
ODD_COLS = MLA_Q_RANK + MLA_KV_RANK + LANES
HY_CT = 128
KV_CHUNK = 512

VMEM_LIMIT_BYTES = 48 * 1024 * 1024


def _params(*sem):
    return pltpu.CompilerParams(dimension_semantics=sem, vmem_limit_bytes=VMEM_LIMIT_BYTES)


def _cond_index(i):
    return (jnp.where(i < N_CTX_TILES, 0, 1 + (i - N_CTX_TILES) // TILES_PER_SEQ), 0, 0)


def _rope_index(i):
    return (jnp.where(i < N_CTX_TILES, 0, 1 + (i - N_CTX_TILES) % TILES_PER_SEQ), 0)


def _ctx_index(i):
    return (jnp.minimum(i, N_CTX_TILES - 1), 0)


def _lat_index(i):
    return (jnp.maximum(i - N_CTX_TILES, 0), 0)


def _const2(i):
    return (0, 0)


def _norm_mod(x, gain, mod_ref, shift_row, scale_row):
    r = lax.rsqrt(jnp.mean(x * x, axis=-1, keepdims=True) + EPS)
    h = (x * r) * gain
    return h * (1.0 + mod_ref[scale_row:scale_row + 1, :]) + mod_ref[shift_row:shift_row + 1, :]


def _slot_norm(t, dims, gain):
    r = lax.rsqrt(jnp.sum(t * t, axis=-1, keepdims=True) / dims + EPS)
    return (t * r) * gain


def _rope(t, tabs, partner):
    cos, sin_up, sin_dn = tabs
    return t * cos + pltpu.roll(t, partner, 1) * sin_up + pltpu.roll(t, LANES - partner, 1) * sin_dn


def _mod_kernel(c_ref, w_ref, b_ref, o_ref):
    c = c_ref[...]
    a = (c * jax.nn.sigmoid(c)).astype(BF16)
    o_ref[...] = jnp.dot(a, w_ref[...].astype(BF16), preferred_element_type=F32) + b_ref[...]


def _modulation(cond, w_mod, b_mod):
    tn = 1536
    return pl.pallas_call(
        _mod_kernel,
        out_shape=jax.ShapeDtypeStruct((DEPTH, COND_ROWS, N_MOD * D_MODEL), F32),
        grid=(DEPTH, (N_MOD * D_MODEL) // tn),
        in_specs=[
            pl.BlockSpec((COND_ROWS, D_MODEL), lambda l, j: (0, 0)),
            pl.BlockSpec((None, D_MODEL, tn), lambda l, j: (l, 0, j)),
            pl.BlockSpec((None, 1, tn), lambda l, j: (l, 0, j)),
        ],
        out_specs=pl.BlockSpec((None, COND_ROWS, tn), lambda l, j: (l, 0, j)),
        compiler_params=_params("parallel", "parallel"),
        name="modulation",
    )(cond, w_mod, b_mod.reshape(DEPTH, 1, N_MOD * D_MODEL))


def _even_in_kernel(x_ref, g_ref, mod_ref, w_ref, qg_ref, kg_ref, rc_ref, ru_ref, rd_ref,
                    uhy_ref, q_ref, kro_ref, vbf_ref, kn_ref, v32_ref):
    h = _norm_mod(x_ref[...], g_ref[...], mod_ref, 0, 1)
    u = jnp.dot(h.astype(BF16), w_ref[...], preferred_element_type=F32)
    uhy_ref[...] = u[:, :HY_IN].astype(BF16)
    tabs = (rc_ref[...], ru_ref[...], rd_ref[...])
    partner = GQA_HEAD_DIM // 4
    qg = qg_ref[...]
    for s in range(GQA_HEADS):
        qs = _slot_norm(u[:, HY_IN + s * LANES:HY_IN + (s + 1) * LANES], GQA_HEAD_DIM, qg)
        q_ref[:, s * LANES:(s + 1) * LANES] = (_rope(qs, tabs, partner) * GQA_HEAD_DIM ** -0.5).astype(BF16)
    k = u[:, HY_IN + GQA_SLOTS:HY_IN + GQA_SLOTS + LANES]
    lo = lax.broadcasted_iota(jnp.int32, k.shape, 1) < GQA_HEAD_DIM
    k2 = k * k
    ss_lo = jnp.sum(jnp.where(lo, k2, 0.0), axis=-1, keepdims=True)
    ss_hi = jnp.sum(jnp.where(lo, 0.0, k2), axis=-1, keepdims=True)
    r = jnp.where(lo, lax.rsqrt(ss_lo / GQA_HEAD_DIM + EPS), lax.rsqrt(ss_hi / GQA_HEAD_DIM + EPS))
    kn = (k * r) * kg_ref[...]
    kn_ref[...] = kn
    kro_ref[...] = _rope(kn, tabs, partner).astype(BF16)
    v = u[:, HY_IN + GQA_SLOTS + LANES:]
    v32_ref[...] = v
    vbf_ref[...] = v.astype(BF16)


def _even_in(x, gain, mod, w, qg, kg, tabs):
    tok = lambda n, dt: jax.ShapeDtypeStruct((T_ALL, n), dt)
    row = lambda n: pl.BlockSpec((TM, n), lambda i: (i, 0))
    tab = pl.BlockSpec((TM, LANES), _rope_index)
    return pl.pallas_call(
        _even_in_kernel,
        out_shape=(tok(HY_IN, BF16), tok(GQA_SLOTS, BF16), tok(LANES, BF16), tok(LANES, BF16),
                   tok(LANES, F32), tok(LANES, F32)),
        grid=(T_ALL // TM,),
        in_specs=[row(D_MODEL), pl.BlockSpec((1, D_MODEL), _const2), pl.BlockSpec((None, N_MOD, D_MODEL), _cond_index),
                  pl.BlockSpec((D_MODEL, EVEN_COLS), _const2), pl.BlockSpec((1, LANES), _const2),
                  pl.BlockSpec((1, LANES), _const2), tab, tab, tab],
        out_specs=(row(HY_IN), row(GQA_SLOTS), row(LANES), row(LANES), row(LANES), row(LANES)),
        compiler_params=_params("parallel"),
        name="even_in",
    )(x, gain.reshape(1, D_MODEL), mod, w, qg, kg, *tabs)


def _mla_heads(qs_all, ks_all, kpe, qg, kg, tabs, q_ref, k_ref):
    partner = MLA_ROPE_DIM // 4
    for s in range(MLA_HEADS):
        sl = slice(s * LANES, (s + 1) * LANES)
        if qs_all is not None:
            qs = _slot_norm(qs_all[:, sl], MLA_QK_DIM, qg)
            q_ref[:, sl] = (_rope(qs, tabs, partner) * MLA_QK_DIM ** -0.5).astype(BF16)
        ks = _slot_norm(ks_all[:, sl] + kpe, MLA_QK_DIM, kg)
        k_ref[:, sl] = (ks if tabs is None else _rope(ks, tabs, partner)).astype(BF16)


def _odd_in_kernel(x_ref, g_ref, mod_ref, win_ref, qln_ref, kvln_ref, wuq_ref, wuk_ref, wuv_ref, qg_ref, kg_ref,
                   rc_ref, ru_ref, rd_ref, q_ref, k_ref, v_ref, ckv_ref, kpe_ref):
    h = _norm_mod(x_ref[...], g_ref[...], mod_ref, 0, 1)
    u = jnp.dot(h.astype(BF16), win_ref[...], preferred_element_type=F32)
    cq = u[:, :MLA_Q_RANK]
    cq = (cq * lax.rsqrt(jnp.mean(cq * cq, axis=-1, keepdims=True) + EPS)) * qln_ref[...]
    ckv = u[:, MLA_Q_RANK:MLA_Q_RANK + MLA_KV_RANK]
    ckv = (ckv * lax.rsqrt(jnp.mean(ckv * ckv, axis=-1, keepdims=True) + EPS)) * kvln_ref[...]
    kpe = u[:, MLA_Q_RANK + MLA_KV_RANK:]
    ckv_ref[...] = ckv
    kpe_ref[...] = kpe
    ckv_b = ckv.astype(BF16)
    v_ref[...] = jnp.dot(ckv_b, wuv_ref[...], preferred_element_type=F32).astype(BF16)
    qs_all = jnp.dot(cq.astype(BF16), wuq_ref[...], preferred_element_type=F32)
    ks_all = jnp.dot(ckv_b, wuk_ref[...], preferred_element_type=F32)
    _mla_heads(qs_all, ks_all, kpe, qg_ref[...], kg_ref[...], (rc_ref[...], ru_ref[...], rd_ref[...]), q_ref, k_ref)


def _odd_in(x, gain, mod, win, qln, kvln, wuq, wuk, wuv, qg, kg, tabs):
    tok = lambda n, dt: jax.ShapeDtypeStruct((T_ALL, n), dt)
    row = lambda n: pl.BlockSpec((TM, n), lambda i: (i, 0))
    full = lambda a: pl.BlockSpec(a.shape, _const2)
    tab = pl.BlockSpec((TM, LANES), _rope_index)
    return pl.pallas_call(
        _odd_in_kernel,
        out_shape=(tok(MLA_SLOTS, BF16), tok(MLA_SLOTS, BF16), tok(MLA_HEADS * MLA_V_DIM, BF16),
                   tok(LANES, F32), tok(LANES, F32)),
        grid=(T_ALL // TM,),
        in_specs=[row(D_MODEL), pl.BlockSpec((1, D_MODEL), _const2), pl.BlockSpec((None, N_MOD, D_MODEL), _cond_index),
                  full(win), full(qln), full(kvln), full(wuq), full(wuk), full(wuv), full(qg), full(kg), tab, tab, tab],
        out_specs=(row(MLA_SLOTS), row(MLA_SLOTS), row(MLA_HEADS * MLA_V_DIM), row(LANES), row(LANES)),
        compiler_params=_params("parallel"),
        name="odd_in",
    )(x, gain.reshape(1, D_MODEL), mod, win, qln, kvln, wuq, wuk, wuv, qg, kg, *tabs)


def _mla_cache_kernel(ckv_ref, kpe_ref, wuk_ref, wuv_ref, kg_ref, k_ref, v_ref):
    ckv_b = ckv_ref[...].astype(BF16)
    v_ref[...] = jnp.dot(ckv_b, wuv_ref[...], preferred_element_type=F32).astype(BF16)
    ks_all = jnp.dot(ckv_b, wuk_ref[...], preferred_element_type=F32)
    _mla_heads(None, ks_all, kpe_ref[...], None, kg_ref[...], None, None, k_ref)


def _mla_cache(ckv, kpe, wuk, wuv, kg):
    rows = ckv.shape[0]
    row = lambda n: pl.BlockSpec((TM, n), lambda i: (i, 0))
    full = lambda a: pl.BlockSpec(a.shape, _const2)
    return pl.pallas_call(
        _mla_cache_kernel,
        out_shape=(jax.ShapeDtypeStruct((rows, MLA_SLOTS), BF16), jax.ShapeDtypeStruct((rows, MLA_HEADS * MLA_V_DIM), BF16)),
        grid=(rows // TM,),
        in_specs=[row(LANES), row(LANES), full(wuk), full(wuv), full(kg)],
        out_specs=(row(MLA_SLOTS), row(MLA_HEADS * MLA_V_DIM)),
        compiler_params=_params("parallel"),
        name="mla_cache",
    )(ckv, kpe, wuk, wuv, kg)


def _out_kernel(*refs, n_parts):
    ctx = refs[:n_parts]
    lat = refs[n_parts:2 * n_parts]
    ws = refs[2 * n_parts:3 * n_parts]
    x_ref, mod_ref, o_ref = refs[3 * n_parts:]
    i = pl.program_id(0)

    def emit(parts):
        y = jnp.dot(parts[0][...], ws[0][...], preferred_element_type=F32)
        for p, w in zip(parts[1:], ws[1:]):
            y += jnp.dot(p[...], w[...], preferred_element_type=F32)
        o_ref[...] = x_ref[...] + mod_ref[2:3, :] * y

    @pl.when(i < N_CTX_TILES)
    def _():
        emit(ctx)

    @pl.when(i >= N_CTX_TILES)
    def _():
        emit(lat)


def _out_proj(parts_ctx, parts_lat, ws, x, mod):
    n_parts = len(ws)
    specs = [pl.BlockSpec((TM, p.shape[1]), _ctx_index) for p in parts_ctx]
    specs += [pl.BlockSpec((TM, p.shape[1]), _lat_index) for p in parts_lat]
    specs += [pl.BlockSpec(w.shape, _const2) for w in ws]
    specs += [pl.BlockSpec((TM, D_MODEL), lambda i: (i, 0)), pl.BlockSpec((None, N_MOD, D_MODEL), _cond_index)]
    return pl.pallas_call(
        functools.partial(_out_kernel, n_parts=n_parts),
        out_shape=jax.ShapeDtypeStruct((T_ALL, D_MODEL), F32),
        grid=(T_ALL // TM,),
        in_specs=specs,
        out_specs=pl.BlockSpec((TM, D_MODEL), lambda i: (i, 0)),
        compiler_params=_params("parallel"),
        name="out_proj",
    )(*parts_ctx, *parts_lat, *ws, x, mod)


def _mm_kernel(a_ref, x_ref, o_ref):
    o_ref[...] = jnp.dot(a_ref[...], x_ref[...], preferred_element_type=F32)


def _matmul(a, x, tn=1024):
    m, k = a.shape
    n = x.shape[1]
    return pl.pallas_call(
        _mm_kernel,
        out_shape=jax.ShapeDtypeStruct((m, n), F32),
        grid=(n // tn,),
        in_specs=[pl.BlockSpec((m, k), _const2), pl.BlockSpec((k, tn), lambda j: (0, j))],
        out_specs=pl.BlockSpec((m, tn), lambda j: (0, j)),
        compiler_params=_params("parallel"),
        name="taps_dft",
    )(a, x)


def _hyena_kernel(v_ref, x1_ref, x2_ref, cw_ref, cb_ref, f_ref, g_ref, tc_ref, ts_ref, h_ref, bias_ref, o_ref):
    n, ct = v_ref.shape
    half = n // 2
    row0 = lax.broadcasted_iota(jnp.int32, (half, ct), 0) == 0
    row_last = lax.broadcasted_iota(jnp.int32, (half, ct), 0) == half - 1
    cos, sin = tc_ref[...], ts_ref[...]

    def conv3(ref, grp):
        s = ref[...].astype(F32)
        se, so = s[:half], s[half:]
        w0, w1, w2 = cw_ref[0, grp:grp + 1, :], cw_ref[1, grp:grp + 1, :], cw_ref[2, grp:grp + 1, :]
        b = cb_ref[grp:grp + 1, :]
        so_prev = jnp.where(row0, 0.0, pltpu.roll(so, 1, 0))
        se_next = jnp.where(row_last, 0.0, pltpu.roll(se, half - 1, 0))
        return (w0 * so_prev + w1 * se + w2 * so + b, w0 * se + w1 * so + w2 * se_next + b)

    def long_conv(ze, zo, order):
        eo = jnp.dot(f_ref[...], jnp.concatenate([ze, zo], axis=1).astype(BF16), preferred_element_type=F32)
        er, ei, orr, oi = eo[:half, :ct], eo[half:, :ct], eo[:half, ct:], eo[half:, ct:]
        har, hai, hbr, hbi = h_ref[order, 0], h_ref[order, 1], h_ref[order, 2], h_ref[order, 3]
        tr = cos * orr + sin * oi
        ti = cos * oi - sin * orr
        pr, pi_, mr, mi = er + tr, ei + ti, er - tr, ei - ti
        ypr = pr * har - pi_ * hai
        ypi = pr * hai + pi_ * har
        ymr = mr * hbr - mi * hbi
        ymi = mr * hbi + mi * hbr
        ar, ai = ypr + ymr, ypi + ymi
        dr, di = ypr - ymr, ypi - ymi
        br = cos * dr - sin * di
        bi = cos * di + sin * dr
        e0, o0, en, on = er[0:1], orr[0:1], ei[0:1], oi[0:1]
        p0 = (e0 + o0) * har[0:1]
        m0 = (e0 - o0) * hbr[0:1]
        ar = jnp.where(row0, p0 + m0, ar)
        br = jnp.where(row0, p0 - m0, br)
        ai = jnp.where(row0, 2.0 * (en * hai[0:1] + on * hbi[0:1]), ai)
        bi = jnp.where(row0, -2.0 * (en * hbi[0:1] - on * hai[0:1]), bi)
        ab = jnp.concatenate([jnp.concatenate([ar, ai], axis=0), jnp.concatenate([br, bi], axis=0)], axis=1)
        y = jnp.dot(g_ref[...], ab.astype(BF16), preferred_element_type=F32)
        bias = bias_ref[order:order + 1, :]
        return y[:, :ct] + bias * ze, y[:, ct:] + bias * zo

    ve, vo = conv3(v_ref, 0)
    x1e, x1o = conv3(x1_ref, 1)
    x2e, x2o = conv3(x2_ref, 2)
    c1e, c1o = long_conv(ve, vo, 0)
    z1e, z1o = x1e * c1e, x1o * c1o
    c2e, c2o = long_conv(z1e, z1o, 1)
    o_ref[:half, :] = (x2e * c2e).astype(o_ref.dtype)
    o_ref[half:, :] = (x2o * c2o).astype(o_ref.dtype)


def _hyena(u_hy, row_block0, n, batch, conv_w, conv_b, dft, twiddle, tables, bias):
    fwd, inv = dft
    half = n // 2
    nj = HY_DIM // HY_CT
    seq = lambda grp: pl.BlockSpec((n, HY_CT), lambda b, j: (row_block0 + b, grp * nj + j))
    return pl.pallas_call(
        _hyena_kernel,
        out_shape=jax.ShapeDtypeStruct((batch * n, HY_DIM), BF16),
        grid=(batch, nj),
        in_specs=[seq(0), seq(1), seq(2),
                  pl.BlockSpec((3, HY_ORDER + 1, HY_CT), lambda b, j: (0, 0, j)),
                  pl.BlockSpec((HY_ORDER + 1, HY_CT), lambda b, j: (0, j)),
                  pl.BlockSpec((n, half), lambda b, j: (0, 0)),
                  pl.BlockSpec((half, n), lambda b, j: (0, 0)),
                  pl.BlockSpec((half, HY_CT), lambda b, j: (0, 0)),
                  pl.BlockSpec((half, HY_CT), lambda b, j: (0, 0)),
                  pl.BlockSpec((HY_ORDER, 4, half, HY_CT), lambda b, j: (0, 0, 0, j)),
                  pl.BlockSpec((HY_ORDER, HY_CT), lambda b, j: (0, j))],
        out_specs=pl.BlockSpec((n, HY_CT), lambda b, j: (b, j)),
        compiler_params=_params("parallel", "parallel"),
        name="hyena",
    )(u_hy, u_hy, u_hy, conv_w.reshape(3, HY_ORDER + 1, HY_DIM), conv_b.reshape(HY_ORDER + 1, HY_DIM),
      fwd, inv, twiddle[0], twiddle[1], tables, bias)


def _attend(q, chunks):
    m = l = acc = None
    for load_k, load_v in chunks:
        s = lax.dot_general(q, load_k(), (((1,), (1,)), ((), ())), preferred_element_type=F32)
        mc = jnp.max(s, axis=-1, keepdims=True)
        if m is None:
            m_new = mc
            p = jnp.exp(s - m_new)
            l = jnp.sum(p, axis=-1, keepdims=True)
            acc = jnp.dot(p.astype(BF16), load_v(), preferred_element_type=F32)
        else:
            m_new = jnp.maximum(m, mc)
            alpha = jnp.exp(m - m_new)
            p = jnp.exp(s - m_new)
            l = alpha * l + jnp.sum(p, axis=-1, keepdims=True)
            acc = alpha * acc + jnp.dot(p.astype(BF16), load_v(), preferred_element_type=F32)
        m = m_new
    return acc / l


def _chunks(refs_kv, col):
    out = []
    for k_ref, v_ref in refs_kv:
        for c in range(k_ref.shape[0] // min(KV_CHUNK, k_ref.shape[0])):
            tk = min(KV_CHUNK, k_ref.shape[0])
            out.append((lambda k_ref=k_ref, c=c, tk=tk: k_ref[c * tk:(c + 1) * tk, col * LANES:(col + 1) * LANES].astype(BF16),
                        lambda v_ref=v_ref, c=c, tk=tk: v_ref[c * tk:(c + 1) * tk, :].astype(BF16)))
    return out


def _gqa_kernel(*refs):
    q_ref, o_ref = refs[0], refs[-1]
    kv = [(refs[i], refs[i + 1]) for i in range(1, len(refs) - 1, 2)]
    tq = q_ref.shape[0]
    for g in range(GQA_KV_HEADS):
        q = jnp.concatenate([q_ref[:, (g * GQA_GROUP + j) * LANES:(g * GQA_GROUP + j + 1) * LANES]
                             for j in range(GQA_GROUP)], axis=0)
        o = _attend(q, _chunks(kv, 0))
        for j in range(GQA_GROUP):
            s = g * GQA_GROUP + j
            o_ref[:, s * LANES:(s + 1) * LANES] = o[j * tq:(j + 1) * tq].astype(o_ref.dtype)


def _gqa_attention(q, row_block0, n, batch, tq, kv_new, kv_cache=None):
    nq = n // tq
    specs = [pl.BlockSpec((tq, GQA_SLOTS), lambda b, i: (row_block0 * nq + b * nq + i, 0))]
    args = [q]
    if kv_cache is not None:
        specs += [pl.BlockSpec((None, PAST_LEN, LANES), lambda b, i: (b, 0, 0))] * 2
        args += list(kv_cache)
    specs += [pl.BlockSpec((n, LANES), lambda b, i: (row_block0 + b, 0))] * 2
    args += list(kv_new)
    return pl.pallas_call(
        _gqa_kernel,
        out_shape=jax.ShapeDtypeStruct((batch * n, GQA_SLOTS), BF16),
        grid=(batch, nq),
        in_specs=specs,
        out_specs=pl.BlockSpec((tq, GQA_SLOTS), lambda b, i: (b * nq + i, 0)),
        compiler_params=_params("parallel", "parallel"),
        name="gqa_attention",
    )(*args)


def _mla_kernel(*refs):
    q_ref, o_ref = refs[0], refs[-1]
    kv = [(refs[i], refs[i + 1]) for i in range(1, len(refs) - 1, 2)]
    outs = [_attend(q_ref[:, hh * LANES:(hh + 1) * LANES], _chunks(kv, hh)) for hh in range(2)]
    low = lax.broadcasted_iota(jnp.int32, outs[0].shape, 1) < MLA_V_DIM
    o_ref[...] = jnp.where(low, outs[0], outs[1]).astype(o_ref.dtype)


def _mla_attention(q, k_new, v_new, row_block0, n, batch, tq, cache=None):
    nq = n // tq
    pairs = MLA_HEADS // 2
    specs = [pl.BlockSpec((tq, 2 * LANES), lambda b, j, i: (row_block0 * nq + b * nq + i, j))]
    args = [q]
    if cache is not None:
        specs += [pl.BlockSpec((PAST_LEN, 2 * LANES), lambda b, j, i: (b, j)),
                  pl.BlockSpec((PAST_LEN, LANES), lambda b, j, i: (b, j))]
        args += list(cache)
    specs += [pl.BlockSpec((n, 2 * LANES), lambda b, j, i: (row_block0 + b, j)),
              pl.BlockSpec((n, LANES), lambda b, j, i: (row_block0 + b, j))]
    args += [k_new, v_new]
    return pl.pallas_call(
        _mla_kernel,
        out_shape=jax.ShapeDtypeStruct((batch * n, MLA_HEADS * MLA_V_DIM), BF16),
        grid=(batch, pairs, nq),
        in_specs=specs,
        out_specs=pl.BlockSpec((tq, LANES), lambda b, j, i: (b * nq + i, j)),
        compiler_params=_params("parallel", "parallel", "parallel"),
        name="mla_attention",
    )(*args)


def _router_kernel(x_ref, g_ref, mod_ref, rw_ref, rb_ref, h_ref, gate_ref):
    h = _norm_mod(x_ref[...], g_ref[...], mod_ref, 3, 4)
    h_ref[...] = h.astype(BF16)
    logits = lax.dot_general(rw_ref[...], h, (((1,), (1,)), ((), ())),
                             precision=lax.Precision.HIGHEST, preferred_element_type=F32)
    score = jax.nn.sigmoid(logits)
    sel = score + rb_ref[...]
    ng = N_GROUPS
    sv = [sel[p * ng:(p + 1) * ng, :] for p in range(EXPERTS_PER_GROUP)]
    hi01, lo01 = jnp.maximum(sv[0], sv[1]), jnp.minimum(sv[0], sv[1])
    hi23, lo23 = jnp.maximum(sv[2], sv[3]), jnp.minimum(sv[2], sv[3])
    top1 = jnp.maximum(hi01, hi23)
    top2 = jnp.maximum(jnp.minimum(hi01, hi23), jnp.maximum(lo01, lo23))
    gscore = top1 + top2
    gidx = lax.broadcasted_iota(jnp.int32, gscore.shape, 0)
    beaten = jnp.zeros(gscore.shape, jnp.int32)
    for j in range(ng):
        other = gscore[j:j + 1, :]
        beaten += ((other > gscore) | ((other == gscore) & (j < gidx))).astype(jnp.int32)
    best = beaten == 0
    chosen = []
    for p in range(EXPERTS_PER_GROUP):
        rank = jnp.zeros(gscore.shape, jnp.int32)
        for pp in range(EXPERTS_PER_GROUP):
            if pp != p:
                wins = (sv[pp] > sv[p]) | ((sv[pp] == sv[p]) & (pp < p))
                rank += wins.astype(jnp.int32)
        chosen.append((rank < 2) & best)
    picked = [jnp.where(chosen[p], score[p * ng:(p + 1) * ng, :], 0.0) for p in range(EXPERTS_PER_GROUP)]
    total = jnp.sum(picked[0] + picked[1] + picked[2] + picked[3], axis=0, keepdims=True)
    gate_ref[...] = jnp.concatenate(picked, axis=0) / total


def _router(x, gain, mod, rw_pm, rb_pm):
    return pl.pallas_call(
        _router_kernel,
        out_shape=(jax.ShapeDtypeStruct((T_ALL, D_MODEL), BF16), jax.ShapeDtypeStruct((N_EXPERTS, T_ALL), F32)),
        grid=(T_ALL // TM,),
        in_specs=[
            pl.BlockSpec((TM, D_MODEL), lambda i: (i, 0)),
            pl.BlockSpec((1, D_MODEL), _const2),
            pl.BlockSpec((None, N_MOD, D_MODEL), _cond_index),
            pl.BlockSpec((N_EXPERTS, D_MODEL), _const2),
            pl.BlockSpec((N_EXPERTS, 1), _const2),
        ],
        out_specs=(pl.BlockSpec((TM, D_MODEL), lambda i: (i, 0)), pl.BlockSpec((N_EXPERTS, TM), lambda i: (0, i))),
        compiler_params=_params("parallel"),
        name="router",
    )(x, gain.reshape(1, D_MODEL), mod, rw_pm, rb_pm)


MOE_TM = 1024


def _moe_kernel(h_ref, gate_ref, wg_ref, wu_ref, wd_ref, x_ref, mod_ref, o_ref, acc_ref):
    e = pl.program_id(1)

    @pl.when(e == 0)
    def _():
        acc_ref[...] = jnp.zeros_like(acc_ref)

    h = h_ref[...]
    a = jnp.dot(h, wg_ref[...].astype(BF16), preferred_element_type=F32)
    u = jnp.dot(h, wu_ref[...].astype(BF16), preferred_element_type=F32)
    gates = gate_ref[...]
    lane = lax.broadcasted_iota(jnp.int32, gates.shape, 1)
    g = jnp.sum(jnp.where(lane == e, gates, 0.0), axis=1, keepdims=True)
    mid = (a * jax.nn.sigmoid(a)) * u * g
    acc_ref[...] += jnp.dot(mid.astype(BF16), wd_ref[...].astype(BF16), preferred_element_type=F32)

    @pl.when(e == N_EXPERTS - 1)
    def _():
        o_ref[...] = x_ref[...] + mod_ref[5:6, :] * acc_ref[...]


def _moe(h, gates, wg, wu, wd, x, mod):
    tm = MOE_TM
    n_ctx = T_CTX // tm
    per_seq = DEC_SEQ // tm
    cond = lambda i, e: (jnp.where(i < n_ctx, 0, 1 + (i - n_ctx) // per_seq), 0, 0)
    return pl.pallas_call(
        _moe_kernel,
        out_shape=jax.ShapeDtypeStruct((T_ALL, D_MODEL), F32),
        grid=(T_ALL // tm, N_EXPERTS),
        in_specs=[
            pl.BlockSpec((tm, D_MODEL), lambda i, e: (i, 0)),
            pl.BlockSpec((tm, N_EXPERTS), lambda i, e: (i, 0)),
            pl.BlockSpec((None, D_MODEL, D_EXPERT), lambda i, e: (e, 0, 0)),
            pl.BlockSpec((None, D_MODEL, D_EXPERT), lambda i, e: (e, 0, 0)),
            pl.BlockSpec((None, D_EXPERT, D_MODEL), lambda i, e: (e, 0, 0)),
            pl.BlockSpec((tm, D_MODEL), lambda i, e: (i, 0)),
            pl.BlockSpec((None, N_MOD, D_MODEL), cond),
        ],
        out_specs=pl.BlockSpec((tm, D_MODEL), lambda i, e: (i, 0)),
        scratch_shapes=[pltpu.VMEM((tm, D_MODEL), F32)],
        compiler_params=_params("parallel", "arbitrary"),
        name="experts",
    )(h, gates, wg, wu, wd, x, mod)


def _split_parity(a, batch, n):
    rest = a.shape[2:]
    return a.reshape(batch, n // 2, 2, *rest).swapaxes(1, 2).reshape(batch * n, *rest)


def _merge_parity(a, batch, n):
    rest = a.shape[1:]
    return a.reshape(batch, 2, n // 2, *rest).swapaxes(1, 2).reshape(batch, n, *rest)


def _dft_matrices(n):
    f = jnp.arange(n, dtype=jnp.int32)[:, None]
    t = jnp.arange(n, dtype=jnp.int32)[None, :]
    ang = ((f * t) % (2 * n)).astype(F32) * (math.pi / n)
    alt = jnp.where(t % 2 == 0, 1.0, -1.0).astype(F32)
    fwd = jnp.concatenate([jnp.cos(ang), alt, -jnp.sin(ang)[1:]], axis=0)
    row = jnp.arange(2 * n, dtype=jnp.int32)[:, None]
    weight = jnp.where((row == 0) | (row == n), 0.5 / n, 1.0 / n).astype(F32)
    inv = (fwd * weight).T
    return fwd, inv


def _conv_constants(n):
    half = n // 2
    fwd, inv = _dft_matrices(half)
    ang = jnp.arange(half, dtype=F32)[:, None] * (math.pi / n)
    cos = jnp.broadcast_to(jnp.cos(ang), (half, HY_CT))
    sin = jnp.broadcast_to(jnp.sin(ang), (half, HY_CT))
    return (fwd.astype(BF16), (0.5 * inv).astype(BF16)), (cos, sin)


def _filter_taps(n, w1, b1, w2, b2, w3, b3, freq):
    hp = lax.Precision.HIGHEST
    t01 = jnp.linspace(0.0, 1.0, n, dtype=F32)
    w = (2.0 * math.pi / n) * jnp.arange(n, dtype=F32)
    bands = jnp.linspace(1e-4, FILT_BANDS - 1, FILT_BANDS, dtype=F32)
    z = jnp.concatenate([t01[:, None], jnp.cos(w[:, None] * bands[None, :]),
                         -jnp.sin(w[:, None] * bands[None, :])], axis=-1)
    h = jnp.sin(freq * (jnp.dot(z, w1, precision=hp) + b1))
    h = jnp.sin(freq * (jnp.dot(h, w2, precision=hp) + b2))
    h = (jnp.dot(h, w3, precision=hp) + b3).reshape(n, HY_ORDER, 2, HY_DIM)
    deltas = jnp.abs(jnp.linspace(HY_MIN_DECAY, HY_MAX_DECAY, HY_DIM, dtype=F32))
    h = h * jnp.exp(-t01[:, None] * deltas[None, :])[:, None, None, :]
    h_fwd = h[:, :, 0]
    h_bwd = h[:, :, 1] * (jnp.arange(n) > 0)[:, None, None].astype(F32)
    scale = lax.rsqrt(jnp.sum(h_fwd * h_fwd, axis=0, keepdims=True) + jnp.sum(h_bwd * h_bwd, axis=0, keepdims=True) + EPS)
    return h_fwd * scale, h_bwd * scale


def _filter_tables(n, fwd, twiddle, filt_args):
    half = n // 2
    h_fwd, h_bwd = _filter_taps(n, *filt_args)
    cols = HY_ORDER * HY_DIM
    taps = jnp.concatenate([h_fwd.reshape(n, cols), h_bwd.reshape(n, cols)], axis=1)
    eo = _matmul(fwd, jnp.concatenate([taps[0::2], taps[1::2]], axis=1).astype(BF16))
    e, o = eo[:, :2 * cols], eo[:, 2 * cols:]
    er, ei, orr, oi = e[:half], e[half:], o[:half], o[half:]
    cos, sin = twiddle[0][:, :1], twiddle[1][:, :1]
    tr = cos * orr + sin * oi
    ti = cos * oi - sin * orr
    f_, b_ = slice(0, cols), slice(cols, 2 * cols)
    har = (er + tr)[:, f_] + (er + tr)[:, b_]
    hai = (ei + ti)[:, f_] - (ei + ti)[:, b_]
    hbr = (er - tr)[:, f_] + (er - tr)[:, b_]
    hbi = (ei - ti)[:, f_] - (ei - ti)[:, b_]
    first = (jnp.arange(half) == 0)[:, None]
    hai = jnp.where(first, ei[0:1, f_] + ei[0:1, b_], hai)
    hbi = jnp.where(first, oi[0:1, b_] - oi[0:1, f_], hbi)
    tabs = jnp.stack([har, hai, hbr, hbi], axis=0).reshape(4, half, HY_ORDER, HY_DIM)
    return tabs.transpose(2, 0, 1, 3)


def _positions(n):
    r = jnp.arange(n, dtype=jnp.int32)
    return jnp.where(r < n // 2, 2 * r, 2 * (r - n // 2) + 1)


def _rope_tables(first_lane, width, period):
    pos = _positions(DEC_SEQ)
    lane = jnp.arange(LANES, dtype=jnp.int32)
    d = lane % period - first_lane
    active = (d >= 0) & (d < width)
    axis_w = width // 2
    npair = axis_w // 2
    e = d % axis_w
    coord = jnp.where(d < axis_w, (pos // GRID_W)[:, None], (pos % GRID_W)[:, None]).astype(F32)
    freqs = ROPE_THETA ** (-(e % npair).astype(F32) / npair)
    ang = coord * freqs[None, :]
    cos = jnp.where(active[None, :], jnp.cos(ang), 1.0)
    sin = jnp.where(active[None, :], jnp.sin(ang), 0.0)
    second = (e >= npair)[None, :]
    sin_up = jnp.where(second, sin, 0.0)
    sin_dn = jnp.where(second, 0.0, -sin)
    ident = lambda v: jnp.full((TM, LANES), v, F32)
    return (jnp.concatenate([ident(1.0), cos], axis=0), jnp.concatenate([ident(0.0), sin_up], axis=0),
            jnp.concatenate([ident(0.0), sin_dn], axis=0))


def _kv_head_of_slot():
    return jax.nn.one_hot(jnp.arange(GQA_HEADS) // GQA_GROUP, GQA_KV_HEADS, dtype=F32)


def _even_weights(w_in, w_out):
    w_hy, w_q, w_k, w_v = jnp.split(w_in, [HY_IN, HY_IN + GQA_Q, HY_IN + GQA_Q + GQA_KV], axis=1)
    sel = _kv_head_of_slot()
    w_q = (w_q.reshape(D_MODEL, GQA_HEADS, 1, GQA_HEAD_DIM) * sel[None, :, :, None]).reshape(D_MODEL, GQA_SLOTS)
    w_in_p = jnp.concatenate([w_hy, w_q, w_k, w_v], axis=1).astype(BF16)
    w_o_hy, w_o_att = w_out[:HY_DIM], w_out[HY_DIM:]
    w_o_att = (w_o_att.reshape(GQA_HEADS, 1, GQA_HEAD_DIM, D_MODEL) * sel[:, :, None, None]).reshape(GQA_SLOTS, D_MODEL)
    return w_in_p, w_o_hy.astype(BF16), w_o_att.astype(BF16)


def _odd_weights(w_in, w_uq, w_ukv, q_norm, k_norm):
    pad = LANES - MLA_QK_DIM
    w_cq, w_ckv, w_kr = jnp.split(w_in, [MLA_Q_RANK, MLA_Q_RANK + MLA_KV_RANK], axis=1)
    w_kr = jnp.pad(w_kr, ((0, 0), (MLA_NOPE_DIM, pad)))
    w_in_p = jnp.concatenate([w_cq, w_ckv, w_kr], axis=1).astype(BF16)
    w_uq_p = jnp.pad(w_uq.reshape(MLA_Q_RANK, MLA_HEADS, MLA_QK_DIM), ((0, 0), (0, 0), (0, pad)))
    w_ukv = w_ukv.reshape(MLA_KV_RANK, MLA_HEADS, MLA_NOPE_DIM + MLA_V_DIM)
    w_uk_p = jnp.pad(w_ukv[:, :, :MLA_NOPE_DIM], ((0, 0), (0, 0), (0, LANES - MLA_NOPE_DIM)))
    w_uv = w_ukv[:, :, MLA_NOPE_DIM:]
    slot_gain = lambda g: jnp.pad(g, (0, pad)).reshape(1, LANES)
    return (w_in_p, w_uq_p.reshape(MLA_Q_RANK, MLA_SLOTS).astype(BF16), w_uk_p.reshape(MLA_KV_RANK, MLA_SLOTS).astype(BF16),
            w_uv.reshape(MLA_KV_RANK, MLA_HEADS * MLA_V_DIM).astype(BF16), slot_gain(q_norm), slot_gain(k_norm))


def kernel(x_prompt, x_sample, cache_gqa_k, cache_gqa_v, cache_mla_ckv, cache_mla_krope, c, c_ctx, norm_mix, norm_ffn, w_mod, b_mod, ev_w_in, ev_conv_w, ev_conv_b, ev_filt_w1, ev_filt_b1, ev_filt_w2, ev_filt_b2, ev_filt_w3, ev_filt_b3, ev_filt_freq, ev_hy_bias, ev_q_norm, ev_k_norm, ev_w_out, od_w_in, od_q_lora_norm, od_kv_lora_norm, od_w_uq, od_w_ukv, od_q_norm, od_k_norm, od_w_out, router_w, router_bias, moe_w_gate, moe_w_up, moe_w_down):
    x = jnp.concatenate([_split_parity(x_prompt, BATCH, SEQ), _split_parity(x_sample, DEC_BATCH, DEC_SEQ)], axis=0)
    cond = jnp.concatenate([c_ctx[None, :], c], axis=0)
    cond = jnp.pad(cond, ((0, COND_ROWS - N_COND), (0, 0)))
    mod_all = _modulation(cond, w_mod, b_mod).reshape(DEPTH, COND_ROWS, N_MOD, D_MODEL)

    perm = jnp.arange(N_EXPERTS).reshape(N_GROUPS, EXPERTS_PER_GROUP).T.reshape(-1)
    rw_pm = router_w.T[perm]
    rb_pm = router_bias[perm].reshape(N_EXPERTS, 1)

    dft_ctx, tw_ctx = _conv_constants(SEQ)
    dft_lat, tw_lat = _conv_constants(DEC_SEQ)
    rope_gqa = _rope_tables(0, GQA_HEAD_DIM, GQA_HEAD_DIM)
    rope_mla = _rope_tables(MLA_NOPE_DIM, MLA_ROPE_DIM, LANES)
    ctx_blocks_lat = T_CTX // DEC_SEQ

    st_k, st_v, st_ckv, st_kr = [], [], [], []
    for l in range(DEPTH):
        i = l // 2
        mod = mod_all[l]
        if l % 2 == 0:
            w_in, w_o_hy, w_o_att = _even_weights(ev_w_in[i], ev_w_out[i])
            both = lambda g: jnp.concatenate([g, g]).reshape(1, LANES)
            u_hy, q, k_ro, v_bf, k_n, v_32 = _even_in(x, norm_mix[l], mod, w_in, both(ev_q_norm[i]), both(ev_k_norm[i]),
                                                      rope_gqa)
            filt_args = (ev_filt_w1[i], ev_filt_b1[i], ev_filt_w2[i], ev_filt_b2[i], ev_filt_w3[i], ev_filt_b3[i],
                         ev_filt_freq[i])
            y_ctx = _hyena(u_hy, 0, SEQ, BATCH, ev_conv_w[i], ev_conv_b[i], dft_ctx, tw_ctx,
                           _filter_tables(SEQ, dft_ctx[0], tw_ctx, filt_args), ev_hy_bias[i])
            y_lat = _hyena(u_hy, ctx_blocks_lat, DEC_SEQ, DEC_BATCH, ev_conv_w[i], ev_conv_b[i], dft_lat, tw_lat,
                           _filter_tables(DEC_SEQ, dft_lat[0], tw_lat, filt_args), ev_hy_bias[i])
            o_ctx = _gqa_attention(q, 0, SEQ, BATCH, SEQ, (k_ro, v_bf))
            cache = (cache_gqa_k[:, i].reshape(DEC_BATCH, PAST_LEN, LANES), cache_gqa_v[:, i].reshape(DEC_BATCH, PAST_LEN, LANES))
            o_lat = _gqa_attention(q, ctx_blocks_lat, DEC_SEQ, DEC_BATCH, 128, (k_ro, v_bf), cache)
            st_k.append(k_n[:T_CTX])
            st_v.append(v_32[:T_CTX])
            x = _out_proj([y_ctx, o_ctx], [y_lat, o_lat], [w_o_hy, w_o_att], x, mod)
        else:
            w_in, w_uq, w_uk, w_uv, qg, kg = _odd_weights(od_w_in[i], od_w_uq[i], od_w_ukv[i], od_q_norm[i], od_k_norm[i])
            q, k, v, ckv, kpe = _odd_in(x, norm_mix[l], mod, w_in, od_q_lora_norm[i].reshape(1, -1),
                                        od_kv_lora_norm[i].reshape(1, -1), w_uq, w_uk, w_uv, qg, kg, rope_mla)
            st_ckv.append(ckv[:T_CTX])
            st_kr.append(kpe[:T_CTX, MLA_NOPE_DIM:MLA_QK_DIM])
            cache_kpe = jnp.pad(cache_mla_krope[:, i].reshape(DEC_BATCH * PAST_LEN, MLA_ROPE_DIM),
                                ((0, 0), (MLA_NOPE_DIM, LANES - MLA_QK_DIM)))
            cache = _mla_cache(cache_mla_ckv[:, i].reshape(DEC_BATCH * PAST_LEN, MLA_KV_RANK), cache_kpe, w_uk, w_uv, kg)
            o_ctx = _mla_attention(q, k, v, 0, SEQ, BATCH, SEQ)
            o_lat = _mla_attention(q, k, v, ctx_blocks_lat, DEC_SEQ, DEC_BATCH, 512, cache)
            x = _out_proj([o_ctx], [o_lat], [od_w_out[i].astype(BF16)], x, mod)

        h, gates_pm = _router(x, norm_ffn[l], mod, rw_pm, rb_pm)
        gates = gates_pm.reshape(EXPERTS_PER_GROUP, N_GROUPS, T_ALL).transpose(2, 1, 0).reshape(T_ALL, N_EXPERTS)
        x = _moe(h, gates, moe_w_gate[l], moe_w_up[l], moe_w_down[l], x, mod)

    y_prompt = _merge_parity(x[:T_CTX], BATCH, SEQ)
    y_sample = _merge_parity(x[T_CTX:], DEC_BATCH, DEC_SEQ)
    state = lambda parts, shape: jnp.stack([_merge_parity(p, BATCH, SEQ) for p in parts], axis=1).reshape(shape)
    return (y_prompt, y_sample,
            state(st_k, (BATCH, DEPTH // 2, SEQ, GQA_KV_HEADS, GQA_HEAD_DIM)),
            state(st_v, (BATCH, DEPTH // 2, SEQ, GQA_KV_HEADS, GQA_HEAD_DIM)),
            state(st_ckv, (BATCH, DEPTH // 2, SEQ, MLA_KV_RANK)),
            state(st_kr, (BATCH, DEPTH // 2, SEQ, MLA_ROPE_DIM)))
```

```python
import functools
import math

import jax
import jax.numpy as jnp
from jax import lax
from jax.experimental import pallas as pl
from jax.experimental.pallas import tpu as pltpu

F32 = jnp.float32
BF16 = jnp.bfloat16

D_MODEL = 1024
BATCH = 16
SEQ = 256
DEPTH = 4
DEC_BATCH = 8
DEC_SEQ = 2048
PAST_LEN = 512
GRID_W = 64
EPS = 1e-6
ROPE_THETA = 10000.0
N_MOD = 6
HY_DIM = D_MODEL // 2
HY_ORDER = 2
FILT_EMB = 33
FILT_BANDS = (FILT_EMB - 1) // 2
HY_MIN_DECAY = math.log(1e-2) / 1.5
HY_MAX_DECAY = math.log(1e-2) / 0.3
GQA_HEADS = 8
GQA_KV_HEADS = 2
GQA_HEAD_DIM = 64
GQA_GROUP = GQA_HEADS // GQA_KV_HEADS
GQA_Q = GQA_HEADS * GQA_HEAD_DIM
GQA_KV = GQA_KV_HEADS * GQA_HEAD_DIM
HY_IN = (HY_ORDER + 1) * HY_DIM
MLA_HEADS = 16
MLA_NOPE_DIM = 64
MLA_ROPE_DIM = 32
MLA_QK_DIM = MLA_NOPE_DIM + MLA_ROPE_DIM
MLA_V_DIM = 64
MLA_Q_RANK = 256
MLA_KV_RANK = 128
N_EXPERTS = 16
N_GROUPS = 4
EXPERTS_PER_GROUP = N_EXPERTS // N_GROUPS
D_EXPERT = 256

LANES = 128
T_CTX = BATCH * SEQ
T_LAT = DEC_BATCH * DEC_SEQ
T_ALL = T_CTX + T_LAT
N_COND = 1 + DEC_BATCH
COND_ROWS = 16
TM = 512
N_CTX_TILES = T_CTX // TM
TILES_PER_SEQ = DEC_SEQ // TM
GQA_SLOTS = GQA_HEADS * LANES
MLA_SLOTS = MLA_HEADS * LANES
EVEN_COLS = HY_IN + GQA_SLOTS + 2 * LANES
ODD_COLS = MLA_Q_RANK + MLA_KV_RANK + LANES
HY_CT = 128
KV_CHUNK = 512

VMEM_LIMIT_BYTES = 48 * 1024 * 1024


def _params(*sem):
    return pltpu.CompilerParams(dimension_semantics=sem, vmem_limit_bytes=VMEM_LIMIT_BYTES)


def _cond_index(i):
    return (jnp.where(i < N_CTX_TILES, 0, 1 + (i - N_CTX_TILES) // TILES_PER_SEQ), 0, 0)


def _rope_index(i):
    return (jnp.where(i < N_CTX_TILES, 0, 1 + (i - N_CTX_TILES) % TILES_PER_SEQ), 0)


def _ctx_index(i):
    return (jnp.minimum(i, N_CTX_TILES - 1), 0)


def _lat_index(i):
    return (jnp.maximum(i - N_CTX_TILES, 0), 0)


def _const2(i):
    return (0, 0)


def _norm_mod(x, gain, mod_ref, shift_row, scale_row):
    r = lax.rsqrt(jnp.mean(x * x, axis=-1, keepdims=True) + EPS)
    h = (x * r) * gain
    return h * (1.0 + mod_ref[scale_row:scale_row + 1, :]) + mod_ref[shift_row:shift_row + 1, :]


def _slot_norm(t, dims, gain):
    r = lax.rsqrt(jnp.sum(t * t, axis=-1, keepdims=True) / dims + EPS)
    return (t * r) * gain


def _rope(t, tabs, partner):
    cos, sin_up, sin_dn = tabs
    return t * cos + pltpu.roll(t, partner, 1) * sin_up + pltpu.roll(t, LANES - partner, 1) * sin_dn


def _mod_kernel(c_ref, w_ref, b_ref, o_ref):
    c = c_ref[...]
    a = (c * jax.nn.sigmoid(c)).astype(BF16)
    o_ref[...] = jnp.dot(a, w_ref[...].astype(BF16), preferred_element_type=F32) + b_ref[...]


def _modulation(cond, w_mod, b_mod):
    tn = 1536
    return pl.pallas_call(
        _mod_kernel,
        out_shape=jax.ShapeDtypeStruct((DEPTH, COND_ROWS, N_MOD * D_MODEL), F32),
        grid=(DEPTH, (N_MOD * D_MODEL) // tn),
        in_specs=[
            pl.BlockSpec((COND_ROWS, D_MODEL), lambda l, j: (0, 0)),
            pl.BlockSpec((None, D_MODEL, tn), lambda l, j: (l, 0, j)),
            pl.BlockSpec((None, 1, tn), lambda l, j: (l, 0, j)),
        ],
        out_specs=pl.BlockSpec((None, COND_ROWS, tn), lambda l, j: (l, 0, j)),
        compiler_params=_params("parallel", "parallel"),
        name="modulation",
    )(cond, w_mod, b_mod.reshape(DEPTH, 1, N_MOD * D_MODEL))


def _even_in_kernel(x_ref, g_ref, mod_ref, w_ref, qg_ref, kg_ref, rc_ref, ru_ref, rd_ref,
                    uhy_ref, q_ref, kro_ref, vbf_ref, kn_ref, v32_ref):
    h = _norm_mod(x_ref[...], g_ref[...], mod_ref, 0, 1)
    u = jnp.dot(h.astype(BF16), w_ref[...], preferred_element_type=F32)
    uhy_ref[...] = u[:, :HY_IN].astype(BF16)
    tabs = (rc_ref[...], ru_ref[...], rd_ref[...])
    partner = GQA_HEAD_DIM // 4
    qg = qg_ref[...]
    for s in range(GQA_HEADS):
        qs = _slot_norm(u[:, HY_IN + s * LANES:HY_IN + (s + 1) * LANES], GQA_HEAD_DIM, qg)
        q_ref[:, s * LANES:(s + 1) * LANES] = (_rope(qs, tabs, partner) * GQA_HEAD_DIM ** -0.5).astype(BF16)
    k = u[:, HY_IN + GQA_SLOTS:HY_IN + GQA_SLOTS + LANES]
    lo = lax.broadcasted_iota(jnp.int32, k.shape, 1) < GQA_HEAD_DIM
    k2 = k * k
    ss_lo = jnp.sum(jnp.where(lo, k2, 0.0), axis=-1, keepdims=True)
    ss_hi = jnp.sum(jnp.where(lo, 0.0, k2), axis=-1, keepdims=True)
    r = jnp.where(lo, lax.rsqrt(ss_lo / GQA_HEAD_DIM + EPS), lax.rsqrt(ss_hi / GQA_HEAD_DIM + EPS))
    kn = (k * r) * kg_ref[...]
    kn_ref[...] = kn
    kro_ref[...] = _rope(kn, tabs, partner).astype(BF16)
    v = u[:, HY_IN + GQA_SLOTS + LANES:]
    v32_ref[...] = v
    vbf_ref[...] = v.astype(BF16)


def _even_in(x, gain, mod, w, qg, kg, tabs):
    tok = lambda n, dt: jax.ShapeDtypeStruct((T_ALL, n), dt)
    row = lambda n: pl.BlockSpec((TM, n), lambda i: (i, 0))
    tab = pl.BlockSpec((TM, LANES), _rope_index)
    return pl.pallas_call(
        _even_in_kernel,
        out_shape=(tok(HY_IN, BF16), tok(GQA_SLOTS, BF16), tok(LANES, BF16), tok(LANES, BF16),
                   tok(LANES, F32), tok(LANES, F32)),
        grid=(T_ALL // TM,),
        in_specs=[row(D_MODEL), pl.BlockSpec((1, D_MODEL), _const2), pl.BlockSpec((None, N_MOD, D_MODEL), _cond_index),
                  pl.BlockSpec((D_MODEL, EVEN_COLS), _const2), pl.BlockSpec((1, LANES), _const2),
                  pl.BlockSpec((1, LANES), _const2), tab, tab, tab],
        out_specs=(row(HY_IN), row(GQA_SLOTS), row(LANES), row(LANES), row(LANES), row(LANES)),
        compiler_params=_params("parallel"),
        name="even_in",
    )(x, gain.reshape(1, D_MODEL), mod, w, qg, kg, *tabs)


def _mla_heads(qs_all, ks_all, kpe, qg, kg, tabs, q_ref, k_ref):
    partner = MLA_ROPE_DIM // 4
    for s in range(MLA_HEADS):
        sl = slice(s * LANES, (s + 1) * LANES)
        if qs_all is not None:
            qs = _slot_norm(qs_all[:, sl], MLA_QK_DIM, qg)
            q_ref[:, sl] = (_rope(qs, tabs, partner) * MLA_QK_DIM ** -0.5).astype(BF16)
        ks = _slot_norm(ks_all[:, sl] + kpe, MLA_QK_DIM, kg)
        k_ref[:, sl] = (ks if tabs is None else _rope(ks, tabs, partner)).astype(BF16)


def _odd_in_kernel(x_ref, g_ref, mod_ref, win_ref, qln_ref, kvln_ref, wuq_ref, wuk_ref, wuv_ref, qg_ref, kg_ref,
                   rc_ref, ru_ref, rd_ref, q_ref, k_ref, v_ref, ckv_ref, kpe_ref):
    h = _norm_mod(x_ref[...], g_ref[...], mod_ref, 0, 1)
    u = jnp.dot(h.astype(BF16), win_ref[...], preferred_element_type=F32)
    cq = u[:, :MLA_Q_RANK]
    cq = (cq * lax.rsqrt(jnp.mean(cq * cq, axis=-1, keepdims=True) + EPS)) * qln_ref[...]
    ckv = u[:, MLA_Q_RANK:MLA_Q_RANK + MLA_KV_RANK]
    ckv = (ckv * lax.rsqrt(jnp.mean(ckv * ckv, axis=-1, keepdims=True) + EPS)) * kvln_ref[...]
    kpe = u[:, MLA_Q_RANK + MLA_KV_RANK:]
    ckv_ref[...] = ckv
    kpe_ref[...] = kpe
    ckv_b = ckv.astype(BF16)
    v_ref[...] = jnp.dot(ckv_b, wuv_ref[...], preferred_element_type=F32).astype(BF16)
    qs_all = jnp.dot(cq.astype(BF16), wuq_ref[...], preferred_element_type=F32)
    ks_all = jnp.dot(ckv_b, wuk_ref[...], preferred_element_type=F32)
    _mla_heads(qs_all, ks_all, kpe, qg_ref[...], kg_ref[...], (rc_ref[...], ru_ref[...], rd_ref[...]), q_ref, k_ref)


def _odd_in(x, gain, mod, win, qln, kvln, wuq, wuk, wuv, qg, kg, tabs):
    tok = lambda n, dt: jax.ShapeDtypeStruct((T_ALL, n), dt)
    row = lambda n: pl.BlockSpec((TM, n), lambda i: (i, 0))
    full = lambda a: pl.BlockSpec(a.shape, _const2)
    tab = pl.BlockSpec((TM, LANES), _rope_index)
    return pl.pallas_call(
        _odd_in_kernel,
        out_shape=(tok(MLA_SLOTS, BF16), tok(MLA_SLOTS, BF16), tok(MLA_HEADS * MLA_V_DIM, BF16),
                   tok(LANES, F32), tok(LANES, F32)),
        grid=(T_ALL // TM,),
        in_specs=[row(D_MODEL), pl.BlockSpec((1, D_MODEL), _const2), pl.BlockSpec((None, N_MOD, D_MODEL), _cond_index),
                  full(win), full(qln), full(kvln), full(wuq), full(wuk), full(wuv), full(qg), full(kg), tab, tab, tab],
        out_specs=(row(MLA_SLOTS), row(MLA_SLOTS), row(MLA_HEADS * MLA_V_DIM), row(LANES), row(LANES)),
        compiler_params=_params("parallel"),
        name="odd_in",
    )(x, gain.reshape(1, D_MODEL), mod, win, qln, kvln, wuq, wuk, wuv, qg, kg, *tabs)


def _mla_cache_kernel(ckv_ref, kpe_ref, wuk_ref, wuv_ref, kg_ref, k_ref, v_ref):
    ckv_b = ckv_ref[...].astype(BF16)
    v_ref[...] = jnp.dot(ckv_b, wuv_ref[...], preferred_element_type=F32).astype(BF16)
    ks_all = jnp.dot(ckv_b, wuk_ref[...], preferred_element_type=F32)
    _mla_heads(None, ks_all, kpe_ref[...], None, kg_ref[...], None, None, k_ref)


def _mla_cache(ckv, kpe, wuk, wuv, kg):
    rows = ckv.shape[0]
    row = lambda n: pl.BlockSpec((TM, n), lambda i: (i, 0))
    full = lambda a: pl.BlockSpec(a.shape, _const2)
    return pl.pallas_call(
        _mla_cache_kernel,
        out_shape=(jax.ShapeDtypeStruct((rows, MLA_SLOTS), BF16), jax.ShapeDtypeStruct((rows, MLA_HEADS * MLA_V_DIM), BF16)),
        grid=(rows // TM,),
        in_specs=[row(LANES), row(LANES), full(wuk), full(wuv), full(kg)],
        out_specs=(row(MLA_SLOTS), row(MLA_HEADS * MLA_V_DIM)),
        compiler_params=_params("parallel"),
        name="mla_cache",
    )(ckv, kpe, wuk, wuv, kg)


def _out_kernel(*refs, n_parts):
    ctx = refs[:n_parts]
    lat = refs[n_parts:2 * n_parts]
    ws = refs[2 * n_parts:3 * n_parts]
    x_ref, mod_ref, o_ref = refs[3 * n_parts:]
    i = pl.program_id(0)

    def emit(parts):
        y = jnp.dot(parts[0][...], ws[0][...], preferred_element_type=F32)
        for p, w in zip(parts[1:], ws[1:]):
            y += jnp.dot(p[...], w[...], preferred_element_type=F32)
        o_ref[...] = x_ref[...] + mod_ref[2:3, :] * y

    @pl.when(i < N_CTX_TILES)
    def _():
        emit(ctx)

    @pl.when(i >= N_CTX_TILES)
    def _():
        emit(lat)


def _out_proj(parts_ctx, parts_lat, ws, x, mod):
    n_parts = len(ws)
    specs = [pl.BlockSpec((TM, p.shape[1]), _ctx_index) for p in parts_ctx]
    specs += [pl.BlockSpec((TM, p.shape[1]), _lat_index) for p in parts_lat]
    specs += [pl.BlockSpec(w.shape, _const2) for w in ws]
    specs += [pl.BlockSpec((TM, D_MODEL), lambda i: (i, 0)), pl.BlockSpec((None, N_MOD, D_MODEL), _cond_index)]
    return pl.pallas_call(
        functools.partial(_out_kernel, n_parts=n_parts),
        out_shape=jax.ShapeDtypeStruct((T_ALL, D_MODEL), F32),
        grid=(T_ALL // TM,),
        in_specs=specs,
        out_specs=pl.BlockSpec((TM, D_MODEL), lambda i: (i, 0)),
        compiler_params=_params("parallel"),
        name="out_proj",
    )(*parts_ctx, *parts_lat, *ws, x, mod)


def _mm_kernel(a_ref, x_ref, o_ref):
    o_ref[...] = jnp.dot(a_ref[...], x_ref[...], preferred_element_type=F32)


def _matmul(a, x, tn=1024):
    m, k = a.shape
    n = x.shape[1]
    return pl.pallas_call(
        _mm_kernel,
        out_shape=jax.ShapeDtypeStruct((m, n), F32),
        grid=(n // tn,),
        in_specs=[pl.BlockSpec((m, k), _const2), pl.BlockSpec((k, tn), lambda j: (0, j))],
        out_specs=pl.BlockSpec((m, tn), lambda j: (0, j)),
        compiler_params=_params("parallel"),
        name="taps_dft",
    )(a, x)


def _hyena_kernel(v_ref, x1_ref, x2_ref, cw_ref, cb_ref, f_ref, g_ref, tc_ref, ts_ref, h_ref, bias_ref, o_ref):
    n, ct = v_ref.shape
    half = n // 2
    row0 = lax.broadcasted_iota(jnp.int32, (half, ct), 0) == 0
    row_last = lax.broadcasted_iota(jnp.int32, (half, ct), 0) == half - 1
    cos, sin = tc_ref[...], ts_ref[...]

    def conv3(ref, grp):
        s = ref[...].astype(F32)
        se, so = s[:half], s[half:]
        w0, w1, w2 = cw_ref[0, grp:grp + 1, :], cw_ref[1, grp:grp + 1, :], cw_ref[2, grp:grp + 1, :]
        b = cb_ref[grp:grp + 1, :]
        so_prev = jnp.where(row0, 0.0, pltpu.roll(so, 1, 0))
        se_next = jnp.where(row_last, 0.0, pltpu.roll(se, half - 1, 0))
        return (w0 * so_prev + w1 * se + w2 * so + b, w0 * se + w1 * so + w2 * se_next + b)

    def long_conv(ze, zo, order):
        eo = jnp.dot(f_ref[...], jnp.concatenate([ze, zo], axis=1).astype(BF16), preferred_element_type=F32)
        er, ei, orr, oi = eo[:half, :ct], eo[half:, :ct], eo[:half, ct:], eo[half:, ct:]
        har, hai, hbr, hbi = h_ref[order, 0], h_ref[order, 1], h_ref[order, 2], h_ref[order, 3]
        tr = cos * orr + sin * oi
        ti = cos * oi - sin * orr
        pr, pi_, mr, mi = er + tr, ei + ti, er - tr, ei - ti
        ypr = pr * har - pi_ * hai
        ypi = pr * hai + pi_ * har
        ymr = mr * hbr - mi * hbi
        ymi = mr * hbi + mi * hbr
        ar, ai = ypr + ymr, ypi + ymi
        dr, di = ypr - ymr, ypi - ymi
        br = cos * dr - sin * di
        bi = cos * di + sin * dr
        e0, o0, en, on = er[0:1], orr[0:1], ei[0:1], oi[0:1]
        p0 = (e0 + o0) * har[0:1]
        m0 = (e0 - o0) * hbr[0:1]
        ar = jnp.where(row0, p0 + m0, ar)
        br = jnp.where(row0, p0 - m0, br)
        ai = jnp.where(row0, 2.0 * (en * hai[0:1] + on * hbi[0:1]), ai)
        bi = jnp.where(row0, -2.0 * (en * hbi[0:1] - on * hai[0:1]), bi)
        ab = jnp.concatenate([jnp.concatenate([ar, ai], axis=0), jnp.concatenate([br, bi], axis=0)], axis=1)
        y = jnp.dot(g_ref[...], ab.astype(BF16), preferred_element_type=F32)
        bias = bias_ref[order:order + 1, :]
        return y[:, :ct] + bias * ze, y[:, ct:] + bias * zo

    ve, vo = conv3(v_ref, 0)
    x1e, x1o = conv3(x1_ref, 1)
    x2e, x2o = conv3(x2_ref, 2)
    c1e, c1o = long_conv(ve, vo, 0)
    z1e, z1o = x1e * c1e, x1o * c1o
    c2e, c2o = long_conv(z1e, z1o, 1)
    o_ref[:half, :] = (x2e * c2e).astype(o_ref.dtype)
    o_ref[half:, :] = (x2o * c2o).astype(o_ref.dtype)


def _hyena(u_hy, row_block0, n, batch, conv_w, conv_b, dft, twiddle, tables, bias):
    fwd, inv = dft
    half = n // 2
    nj = HY_DIM // HY_CT
    seq = lambda grp: pl.BlockSpec((n, HY_CT), lambda b, j: (row_block0 + b, grp * nj + j))
    return pl.pallas_call(
        _hyena_kernel,
        out_shape=jax.ShapeDtypeStruct((batch * n, HY_DIM), BF16),
        grid=(batch, nj),
        in_specs=[seq(0), seq(1), seq(2),
                  pl.BlockSpec((3, HY_ORDER + 1, HY_CT), lambda b, j: (0, 0, j)),
                  pl.BlockSpec((HY_ORDER + 1, HY_CT), lambda b, j: (0, j)),
                  pl.BlockSpec((n, half), lambda b, j: (0, 0)),
                  pl.BlockSpec((half, n), lambda b, j: (0, 0)),
                  pl.BlockSpec((half, HY_CT), lambda b, j: (0, 0)),
                  pl.BlockSpec((half, HY_CT), lambda b, j: (0, 0)),
                  pl.BlockSpec((HY_ORDER, 4, half, HY_CT), lambda b, j: (0, 0, 0, j)),
                  pl.BlockSpec((HY_ORDER, HY_CT), lambda b, j: (0, j))],
        out_specs=pl.BlockSpec((n, HY_CT), lambda b, j: (b, j)),
        compiler_params=_params("parallel", "parallel"),
        name="hyena",
    )(u_hy, u_hy, u_hy, conv_w.reshape(3, HY_ORDER + 1, HY_DIM), conv_b.reshape(HY_ORDER + 1, HY_DIM),
      fwd, inv, twiddle[0], twiddle[1], tables, bias)


def _attend(q, chunks):
    m = l = acc = None
    for load_k, load_v in chunks:
        s = lax.dot_general(q, load_k(), (((1,), (1,)), ((), ())), preferred_element_type=F32)
        mc = jnp.max(s, axis=-1, keepdims=True)
        if m is None:
            m_new = mc
            p = jnp.exp(s - m_new)
            l = jnp.sum(p, axis=-1, keepdims=True)
            acc = jnp.dot(p.astype(BF16), load_v(), preferred_element_type=F32)
        else:
            m_new = jnp.maximum(m, mc)
            alpha = jnp.exp(m - m_new)
            p = jnp.exp(s - m_new)
            l = alpha * l + jnp.sum(p, axis=-1, keepdims=True)
            acc = alpha * acc + jnp.dot(p.astype(BF16), load_v(), preferred_element_type=F32)
        m = m_new
    return acc / l


def _chunks(refs_kv, col):
    out = []
    for k_ref, v_ref in refs_kv:
        for c in range(k_ref.shape[0] // min(KV_CHUNK, k_ref.shape[0])):
            tk = min(KV_CHUNK, k_ref.shape[0])
            out.append((lambda k_ref=k_ref, c=c, tk=tk: k_ref[c * tk:(c + 1) * tk, col * LANES:(col + 1) * LANES].astype(BF16),
                        lambda v_ref=v_ref, c=c, tk=tk: v_ref[c * tk:(c + 1) * tk, :].astype(BF16)))
    return out


def _gqa_kernel(*refs):
    q_ref, o_ref = refs[0], refs[-1]
    kv = [(refs[i], refs[i + 1]) for i in range(1, len(refs) - 1, 2)]
    tq = q_ref.shape[0]
    for g in range(GQA_KV_HEADS):
        q = jnp.concatenate([q_ref[:, (g * GQA_GROUP + j) * LANES:(g * GQA_GROUP + j + 1) * LANES]
                             for j in range(GQA_GROUP)], axis=0)
        o = _attend(q, _chunks(kv, 0))
        for j in range(GQA_GROUP):
            s = g * GQA_GROUP + j
            o_ref[:, s * LANES:(s + 1) * LANES] = o[j * tq:(j + 1) * tq].astype(o_ref.dtype)


def _gqa_attention(q, row_block0, n, batch, tq, kv_new, kv_cache=None):
    nq = n // tq
    specs = [pl.BlockSpec((tq, GQA_SLOTS), lambda b, i: (row_block0 * nq + b * nq + i, 0))]
    args = [q]
    if kv_cache is not None:
        specs += [pl.BlockSpec((None, PAST_LEN, LANES), lambda b, i: (b, 0, 0))] * 2
        args += list(kv_cache)
    specs += [pl.BlockSpec((n, LANES), lambda b, i: (row_block0 + b, 0))] * 2
    args += list(kv_new)
    return pl.pallas_call(
        _gqa_kernel,
        out_shape=jax.ShapeDtypeStruct((batch * n, GQA_SLOTS), BF16),
        grid=(batch, nq),
        in_specs=specs,
        out_specs=pl.BlockSpec((tq, GQA_SLOTS), lambda b, i: (b * nq + i, 0)),
        compiler_params=_params("parallel", "parallel"),
        name="gqa_attention",
    )(*args)


def _mla_kernel(*refs):
    q_ref, o_ref = refs[0], refs[-1]
    kv = [(refs[i], refs[i + 1]) for i in range(1, len(refs) - 1, 2)]
    outs = [_attend(q_ref[:, hh * LANES:(hh + 1) * LANES], _chunks(kv, hh)) for hh in range(2)]
    low = lax.broadcasted_iota(jnp.int32, outs[0].shape, 1) < MLA_V_DIM
    o_ref[...] = jnp.where(low, outs[0], outs[1]).astype(o_ref.dtype)


def _mla_attention(q, k_new, v_new, row_block0, n, batch, tq, cache=None):
    nq = n // tq
    pairs = MLA_HEADS // 2
    specs = [pl.BlockSpec((tq, 2 * LANES), lambda b, j, i: (row_block0 * nq + b * nq + i, j))]
    args = [q]
    if cache is not None:
        specs += [pl.BlockSpec((PAST_LEN, 2 * LANES), lambda b, j, i: (b, j)),
                  pl.BlockSpec((PAST_LEN, LANES), lambda b, j, i: (b, j))]
        args += list(cache)
    specs += [pl.BlockSpec((n, 2 * LANES), lambda b, j, i: (row_block0 + b, j)),
              pl.BlockSpec((n, LANES), lambda b, j, i: (row_block0 + b, j))]
    args += [k_new, v_new]
    return pl.pallas_call(
        _mla_kernel,
        out_shape=jax.ShapeDtypeStruct((batch * n, MLA_HEADS * MLA_V_DIM), BF16),
        grid=(batch, pairs, nq),
        in_specs=specs,
        out_specs=pl.BlockSpec((tq, LANES), lambda b, j, i: (b * nq + i, j)),
        compiler_params=_params("parallel", "parallel", "parallel"),
        name="mla_attention",
    )(*args)


def _router_kernel(x_ref, g_ref, mod_ref, rw_ref, rb_ref, h_ref, route_ref):
    h = _norm_mod(x_ref[...], g_ref[...], mod_ref, 3, 4)
    for cc in range(D_MODEL // LANES):
        h_ref[:, cc, :] = h[:, cc * LANES:(cc + 1) * LANES]
    logits = lax.dot_general(rw_ref[...], h, (((1,), (1,)), ((), ())),
                             precision=lax.Precision.HIGHEST, preferred_element_type=F32)
    score = jax.nn.sigmoid(logits)
    sel = score + rb_ref[...]
    ng = N_GROUPS
    sv = [sel[p * ng:(p + 1) * ng, :] for p in range(EXPERTS_PER_GROUP)]
    hi01, lo01 = jnp.maximum(sv[0], sv[1]), jnp.minimum(sv[0], sv[1])
    hi23, lo23 = jnp.maximum(sv[2], sv[3]), jnp.minimum(sv[2], sv[3])
    top1 = jnp.maximum(hi01, hi23)
    top2 = jnp.maximum(jnp.minimum(hi01, hi23), jnp.maximum(lo01, lo23))
    gscore = top1 + top2
    gidx = lax.broadcasted_iota(jnp.int32, gscore.shape, 0)
    beaten = jnp.zeros(gscore.shape, jnp.int32)
    for j in range(ng):
        other = gscore[j:j + 1, :]
        beaten += ((other > gscore) | ((other == gscore) & (j < gidx))).astype(jnp.int32)
    best = beaten == 0
    chosen = []
    for p in range(EXPERTS_PER_GROUP):
        rank = jnp.zeros(gscore.shape, jnp.int32)
        for pp in range(EXPERTS_PER_GROUP):
            if pp != p:
                wins = (sv[pp] > sv[p]) | ((sv[pp] == sv[p]) & (pp < p))
                rank += wins.astype(jnp.int32)
        chosen.append((rank < 2) & best)
    picked = [jnp.where(chosen[p], score[p * ng:(p + 1) * ng, :], 0.0) for p in range(EXPERTS_PER_GROUP)]
    total = jnp.sum(picked[0] + picked[1] + picked[2] + picked[3], axis=0, keepdims=True)
    rows = [jnp.sum(picked[p], axis=0, keepdims=True) / total for p in range(EXPERTS_PER_GROUP)]
    rows.append(jnp.sum(jnp.where(best, gidx, 0), axis=0, keepdims=True).astype(F32))
    rows.append(jnp.zeros((ROUTE_ROWS - EXPERTS_PER_GROUP - 1, total.shape[1]), F32))
    route_ref[...] = jnp.concatenate(rows, axis=0)


def _router(x, gain, mod, rw_pm, rb_pm):
    return pl.pallas_call(
        _router_kernel,
        out_shape=(jax.ShapeDtypeStruct((T_ALL, D_MODEL // LANES, LANES), F32),
                   jax.ShapeDtypeStruct((ROUTE_ROWS, T_ALL), F32)),
        grid=(T_ALL // TM,),
        in_specs=[
            pl.BlockSpec((TM, D_MODEL), lambda i: (i, 0)),
            pl.BlockSpec((1, D_MODEL), _const2),
            pl.BlockSpec((None, N_MOD, D_MODEL), _cond_index),
            pl.BlockSpec((N_EXPERTS, D_MODEL), _const2),
            pl.BlockSpec((N_EXPERTS, 1), _const2),
        ],
        out_specs=(pl.BlockSpec((TM, D_MODEL // LANES, LANES), lambda i: (i, 0, 0)),
                   pl.BlockSpec((ROUTE_ROWS, TM), lambda i: (0, i))),
        compiler_params=_params("parallel"),
        name="router",
    )(x, gain.reshape(1, D_MODEL), mod, rw_pm, rb_pm)


MOE_TG = 512
MOE_NT = T_ALL // MOE_TG + N_GROUPS
GROUP_FF = EXPERTS_PER_GROUP * D_EXPERT
ROUTE_ROWS = 8


def _moe_kernel(tg_ref, src_ref, dst_ref, h_hbm, gate_ref, wg_ref, wu_ref, wd_ref, y_hbm, hbuf, ybuf, gsem, ssem):
    j = pl.program_id(0)
    slot = j % 2
    other = 1 - slot

    def gather(tile, s):
        for r in range(MOE_TG):
            pltpu.make_async_copy(h_hbm.at[src_ref[tile * MOE_TG + r]], hbuf.at[s, r], gsem.at[s]).start()

    def scatter(tile, s):
        for r in range(MOE_TG):
            pltpu.make_async_copy(ybuf.at[s, r], y_hbm.at[dst_ref[tile * MOE_TG + r]], ssem.at[s]).start()

    def wait_gather(s):
        pltpu.make_async_copy(h_hbm.at[pl.ds(0, MOE_TG)], hbuf.at[s], gsem.at[s]).wait()

    def wait_scatter(s):
        pltpu.make_async_copy(ybuf.at[s], y_hbm.at[pl.ds(0, MOE_TG)], ssem.at[s]).wait()

    @pl.when(j == 0)
    def _():
        gather(0, 0)
        ybuf[...] = jnp.zeros_like(ybuf)
        for s in range(2):
            spare = pltpu.make_async_copy(ybuf.at[s], y_hbm.at[pl.ds(T_ALL + s * MOE_TG, MOE_TG)], ssem.at[s])
            spare.start()
            spare.wait()

    gather(j + 1, other)
    wait_gather(slot)
    nc = D_MODEL // LANES
    h = jnp.concatenate([hbuf[slot, :, cc, :] for cc in range(nc)], axis=1).astype(BF16)
    a = jnp.dot(h, wg_ref[...], preferred_element_type=F32)
    u = jnp.dot(h, wu_ref[...], preferred_element_type=F32)
    gates = gate_ref[...]
    mid = (a * jax.nn.sigmoid(a)) * u
    mid = jnp.concatenate([mid[:, p * D_EXPERT:(p + 1) * D_EXPERT] * gates[:, p:p + 1]
                           for p in range(EXPERTS_PER_GROUP)], axis=1)
    y = jnp.dot(mid.astype(BF16), wd_ref[...], preferred_element_type=F32)

    @pl.when(j >= 2)
    def _():
        wait_scatter(slot)

    for cc in range(nc):
        ybuf[slot, :, cc, :] = y[:, cc * LANES:(cc + 1) * LANES]
    scatter(j, slot)

    @pl.when(j == MOE_NT - 1)
    def _():
        wait_gather(other)
        wait_scatter(other)
        wait_scatter(slot)


def _moe(h, tile_group, src_rows, dst_rows, gates_sorted, wg, wu, wd):
    wspec = lambda k, n: pl.BlockSpec((None, k, n), lambda j, tg, src, dst: (tg[j], 0, 0))
    return pl.pallas_call(
        _moe_kernel,
        out_shape=jax.ShapeDtypeStruct((T_ALL + 2 * MOE_TG, D_MODEL // LANES, LANES), F32),
        grid_spec=pltpu.PrefetchScalarGridSpec(
            num_scalar_prefetch=3,
            grid=(MOE_NT,),
            in_specs=[pl.BlockSpec(memory_space=pl.ANY),
                      pl.BlockSpec((MOE_TG, EXPERTS_PER_GROUP), lambda j, tg, src, dst: (j, 0)),
                      wspec(D_MODEL, GROUP_FF), wspec(D_MODEL, GROUP_FF), wspec(GROUP_FF, D_MODEL)],
            out_specs=pl.BlockSpec(memory_space=pl.ANY),
            scratch_shapes=[pltpu.VMEM((2, MOE_TG, D_MODEL // LANES, LANES), F32),
                            pltpu.VMEM((2, MOE_TG, D_MODEL // LANES, LANES), F32),
                            pltpu.SemaphoreType.DMA((2,)), pltpu.SemaphoreType.DMA((2,))]),
        compiler_params=_params("arbitrary"),
        name="experts",
    )(tile_group, src_rows, dst_rows, h, gates_sorted, wg, wu, wd)


def _residual_kernel(x_ref, y_ref, mod_ref, o_ref):
    y = jnp.concatenate([y_ref[:, cc, :] for cc in range(D_MODEL // LANES)], axis=1)
    o_ref[...] = x_ref[...] + mod_ref[5:6, :] * y


def _residual(x, y, mod):
    return pl.pallas_call(
        _residual_kernel,
        out_shape=jax.ShapeDtypeStruct((T_ALL, D_MODEL), F32),
        grid=(T_ALL // TM,),
        in_specs=[pl.BlockSpec((TM, D_MODEL), lambda i: (i, 0)),
                  pl.BlockSpec((TM, D_MODEL // LANES, LANES), lambda i: (i, 0, 0)),
                  pl.BlockSpec((None, N_MOD, D_MODEL), _cond_index)],
        out_specs=pl.BlockSpec((TM, D_MODEL), lambda i: (i, 0)),
        compiler_params=_params("parallel"),
        name="moe_residual",
    )(x, y, mod)


def _route_tables(route):
    gid = route[EXPERTS_PER_GROUP].astype(jnp.int32)
    onehot = (gid[:, None] == jnp.arange(N_GROUPS, dtype=jnp.int32)[None, :]).astype(jnp.int32)
    count = jnp.sum(onehot, axis=0)
    rank = jnp.sum((jnp.cumsum(onehot, axis=0) - onehot) * onehot, axis=1)
    padded = ((count + MOE_TG - 1) // MOE_TG) * MOE_TG
    ends = jnp.cumsum(padded)
    dest = (ends - padded)[gid] + rank
    rows = MOE_NT * MOE_TG
    token = jnp.arange(T_ALL, dtype=jnp.int32)
    src_rows = jnp.zeros((rows + MOE_TG,), jnp.int32).at[dest].set(token)
    r = jnp.arange(rows, dtype=jnp.int32)
    spare = T_ALL + ((r // MOE_TG) % 2) * MOE_TG + r % MOE_TG
    dst_rows = spare.at[dest].set(token)
    gates_sorted = jnp.zeros((rows, EXPERTS_PER_GROUP), F32).at[dest].set(route[:EXPERTS_PER_GROUP].T)
    tile_start = jnp.arange(MOE_NT, dtype=jnp.int32) * MOE_TG
    tile_group = jnp.minimum(jnp.sum((tile_start[:, None] >= ends[None, :]).astype(jnp.int32), axis=1), N_GROUPS - 1)
    return tile_group, src_rows, dst_rows, gates_sorted


def _group_weights(wg, wu, wd):
    wide = lambda w: (w.reshape(N_GROUPS, EXPERTS_PER_GROUP, D_MODEL, D_EXPERT).transpose(0, 2, 1, 3)
                      .reshape(N_GROUPS, D_MODEL, GROUP_FF).astype(BF16))
    return wide(wg), wide(wu), wd.reshape(N_GROUPS, GROUP_FF, D_MODEL).astype(BF16)


def _split_parity(a, batch, n):
    rest = a.shape[2:]
    return a.reshape(batch, n // 2, 2, *rest).swapaxes(1, 2).reshape(batch * n, *rest)


def _merge_parity(a, batch, n):
    rest = a.shape[1:]
    return a.reshape(batch, 2, n // 2, *rest).swapaxes(1, 2).reshape(batch, n, *rest)


def _dft_matrices(n):
    f = jnp.arange(n, dtype=jnp.int32)[:, None]
    t = jnp.arange(n, dtype=jnp.int32)[None, :]
    ang = ((f * t) % (2 * n)).astype(F32) * (math.pi / n)
    alt = jnp.where(t % 2 == 0, 1.0, -1.0).astype(F32)
    fwd = jnp.concatenate([jnp.cos(ang), alt, -jnp.sin(ang)[1:]], axis=0)
    row = jnp.arange(2 * n, dtype=jnp.int32)[:, None]
    weight = jnp.where((row == 0) | (row == n), 0.5 / n, 1.0 / n).astype(F32)
    inv = (fwd * weight).T
    return fwd, inv


def _conv_constants(n):
    half = n // 2
    fwd, inv = _dft_matrices(half)
    ang = jnp.arange(half, dtype=F32)[:, None] * (math.pi / n)
    cos = jnp.broadcast_to(jnp.cos(ang), (half, HY_CT))
    sin = jnp.broadcast_to(jnp.sin(ang), (half, HY_CT))
    return (fwd.astype(BF16), (0.5 * inv).astype(BF16)), (cos, sin)


def _filter_taps(n, w1, b1, w2, b2, w3, b3, freq):
    hp = lax.Precision.HIGHEST
    t01 = jnp.linspace(0.0, 1.0, n, dtype=F32)
    w = (2.0 * math.pi / n) * jnp.arange(n, dtype=F32)
    bands = jnp.linspace(1e-4, FILT_BANDS - 1, FILT_BANDS, dtype=F32)
    z = jnp.concatenate([t01[:, None], jnp.cos(w[:, None] * bands[None, :]),
                         -jnp.sin(w[:, None] * bands[None, :])], axis=-1)
    h = jnp.sin(freq * (jnp.dot(z, w1, precision=hp) + b1))
    h = jnp.sin(freq * (jnp.dot(h, w2, precision=hp) + b2))
    h = (jnp.dot(h, w3, precision=hp) + b3).reshape(n, HY_ORDER, 2, HY_DIM)
    deltas = jnp.abs(jnp.linspace(HY_MIN_DECAY, HY_MAX_DECAY, HY_DIM, dtype=F32))
    h = h * jnp.exp(-t01[:, None] * deltas[None, :])[:, None, None, :]
    h_fwd = h[:, :, 0]
    h_bwd = h[:, :, 1] * (jnp.arange(n) > 0)[:, None, None].astype(F32)
    scale = lax.rsqrt(jnp.sum(h_fwd * h_fwd, axis=0, keepdims=True) + jnp.sum(h_bwd * h_bwd, axis=0, keepdims=True) + EPS)
    return h_fwd * scale, h_bwd * scale


def _filter_tables(n, fwd, twiddle, filt_args):
    half = n // 2
    h_fwd, h_bwd = _filter_taps(n, *filt_args)
    cols = HY_ORDER * HY_DIM
    taps = jnp.concatenate([h_fwd.reshape(n, cols), h_bwd.reshape(n, cols)], axis=1)
    eo = _matmul(fwd, jnp.concatenate([taps[0::2], taps[1::2]], axis=1).astype(BF16))
    e, o = eo[:, :2 * cols], eo[:, 2 * cols:]
    er, ei, orr, oi = e[:half], e[half:], o[:half], o[half:]
    cos, sin = twiddle[0][:, :1], twiddle[1][:, :1]
    tr = cos * orr + sin * oi
    ti = cos * oi - sin * orr
    f_, b_ = slice(0, cols), slice(cols, 2 * cols)
    har = (er + tr)[:, f_] + (er + tr)[:, b_]
    hai = (ei + ti)[:, f_] - (ei + ti)[:, b_]
    hbr = (er - tr)[:, f_] + (er - tr)[:, b_]
    hbi = (ei - ti)[:, f_] - (ei - ti)[:, b_]
    first = (jnp.arange(half) == 0)[:, None]
    hai = jnp.where(first, ei[0:1, f_] + ei[0:1, b_], hai)
    hbi = jnp.where(first, oi[0:1, b_] - oi[0:1, f_], hbi)
    tabs = jnp.stack([har, hai, hbr, hbi], axis=0).reshape(4, half, HY_ORDER, HY_DIM)
    return tabs.transpose(2, 0, 1, 3)


def _positions(n):
    r = jnp.arange(n, dtype=jnp.int32)
    return jnp.where(r < n // 2, 2 * r, 2 * (r - n // 2) + 1)


def _rope_tables(first_lane, width, period):
    pos = _positions(DEC_SEQ)
    lane = jnp.arange(LANES, dtype=jnp.int32)
    d = lane % period - first_lane
    active = (d >= 0) & (d < width)
    axis_w = width // 2
    npair = axis_w // 2
    e = d % axis_w
    coord = jnp.where(d < axis_w, (pos // GRID_W)[:, None], (pos % GRID_W)[:, None]).astype(F32)
    freqs = ROPE_THETA ** (-(e % npair).astype(F32) / npair)
    ang = coord * freqs[None, :]
    cos = jnp.where(active[None, :], jnp.cos(ang), 1.0)
    sin = jnp.where(active[None, :], jnp.sin(ang), 0.0)
    second = (e >= npair)[None, :]
    sin_up = jnp.where(second, sin, 0.0)
    sin_dn = jnp.where(second, 0.0, -sin)
    ident = lambda v: jnp.full((TM, LANES), v, F32)
    return (jnp.concatenate([ident(1.0), cos], axis=0), jnp.concatenate([ident(0.0), sin_up], axis=0),
            jnp.concatenate([ident(0.0), sin_dn], axis=0))


def _kv_head_of_slot():
    return jax.nn.one_hot(jnp.arange(GQA_HEADS) // GQA_GROUP, GQA_KV_HEADS, dtype=F32)


def _even_weights(w_in, w_out):
    w_hy, w_q, w_k, w_v = jnp.split(w_in, [HY_IN, HY_IN + GQA_Q, HY_IN + GQA_Q + GQA_KV], axis=1)
    sel = _kv_head_of_slot()
    w_q = (w_q.reshape(D_MODEL, GQA_HEADS, 1, GQA_HEAD_DIM) * sel[None, :, :, None]).reshape(D_MODEL, GQA_SLOTS)
    w_in_p = jnp.concatenate([w_hy, w_q, w_k, w_v], axis=1).astype(BF16)
    w_o_hy, w_o_att = w_out[:HY_DIM], w_out[HY_DIM:]
    w_o_att = (w_o_att.reshape(GQA_HEADS, 1, GQA_HEAD_DIM, D_MODEL) * sel[:, :, None, None]).reshape(GQA_SLOTS, D_MODEL)
    return w_in_p, w_o_hy.astype(BF16), w_o_att.astype(BF16)


def _odd_weights(w_in, w_uq, w_ukv, q_norm, k_norm):
    pad = LANES - MLA_QK_DIM
    w_cq, w_ckv, w_kr = jnp.split(w_in, [MLA_Q_RANK, MLA_Q_RANK + MLA_KV_RANK], axis=1)
    w_kr = jnp.pad(w_kr, ((0, 0), (MLA_NOPE_DIM, pad)))
    w_in_p = jnp.concatenate([w_cq, w_ckv, w_kr], axis=1).astype(BF16)
    w_uq_p = jnp.pad(w_uq.reshape(MLA_Q_RANK, MLA_HEADS, MLA_QK_DIM), ((0, 0), (0, 0), (0, pad)))
    w_ukv = w_ukv.reshape(MLA_KV_RANK, MLA_HEADS, MLA_NOPE_DIM + MLA_V_DIM)
    w_uk_p = jnp.pad(w_ukv[:, :, :MLA_NOPE_DIM], ((0, 0), (0, 0), (0, LANES - MLA_NOPE_DIM)))
    w_uv = w_ukv[:, :, MLA_NOPE_DIM:]
    slot_gain = lambda g: jnp.pad(g, (0, pad)).reshape(1, LANES)
    return (w_in_p, w_uq_p.reshape(MLA_Q_RANK, MLA_SLOTS).astype(BF16), w_uk_p.reshape(MLA_KV_RANK, MLA_SLOTS).astype(BF16),
            w_uv.reshape(MLA_KV_RANK, MLA_HEADS * MLA_V_DIM).astype(BF16), slot_gain(q_norm), slot_gain(k_norm))


def kernel(x_prompt, x_sample, cache_gqa_k, cache_gqa_v, cache_mla_ckv, cache_mla_krope, c, c_ctx, norm_mix, norm_ffn, w_mod, b_mod, ev_w_in, ev_conv_w, ev_conv_b, ev_filt_w1, ev_filt_b1, ev_filt_w2, ev_filt_b2, ev_filt_w3, ev_filt_b3, ev_filt_freq, ev_hy_bias, ev_q_norm, ev_k_norm, ev_w_out, od_w_in, od_q_lora_norm, od_kv_lora_norm, od_w_uq, od_w_ukv, od_q_norm, od_k_norm, od_w_out, router_w, router_bias, moe_w_gate, moe_w_up, moe_w_down):
    x = jnp.concatenate([_split_parity(x_prompt, BATCH, SEQ), _split_parity(x_sample, DEC_BATCH, DEC_SEQ)], axis=0)
    cond = jnp.concatenate([c_ctx[None, :], c], axis=0)
    cond = jnp.pad(cond, ((0, COND_ROWS - N_COND), (0, 0)))
    mod_all = _modulation(cond, w_mod, b_mod).reshape(DEPTH, COND_ROWS, N_MOD, D_MODEL)

    perm = jnp.arange(N_EXPERTS).reshape(N_GROUPS, EXPERTS_PER_GROUP).T.reshape(-1)
    rw_pm = router_w.T[perm]
    rb_pm = router_bias[perm].reshape(N_EXPERTS, 1)

    dft_ctx, tw_ctx = _conv_constants(SEQ)
    dft_lat, tw_lat = _conv_constants(DEC_SEQ)
    rope_gqa = _rope_tables(0, GQA_HEAD_DIM, GQA_HEAD_DIM)
    rope_mla = _rope_tables(MLA_NOPE_DIM, MLA_ROPE_DIM, LANES)
    ctx_blocks_lat = T_CTX // DEC_SEQ

    st_k, st_v, st_ckv, st_kr = [], [], [], []
    for l in range(DEPTH):
        i = l // 2
        mod = mod_all[l]
        if l % 2 == 0:
            w_in, w_o_hy, w_o_att = _even_weights(ev_w_in[i], ev_w_out[i])
            both = lambda g: jnp.concatenate([g, g]).reshape(1, LANES)
            u_hy, q, k_ro, v_bf, k_n, v_32 = _even_in(x, norm_mix[l], mod, w_in, both(ev_q_norm[i]), both(ev_k_norm[i]),
                                                      rope_gqa)
            filt_args = (ev_filt_w1[i], ev_filt_b1[i], ev_filt_w2[i], ev_filt_b2[i], ev_filt_w3[i], ev_filt_b3[i],
                         ev_filt_freq[i])
            y_ctx = _hyena(u_hy, 0, SEQ, BATCH, ev_conv_w[i], ev_conv_b[i], dft_ctx, tw_ctx,
                           _filter_tables(SEQ, dft_ctx[0], tw_ctx, filt_args), ev_hy_bias[i])
            y_lat = _hyena(u_hy, ctx_blocks_lat, DEC_SEQ, DEC_BATCH, ev_conv_w[i], ev_conv_b[i], dft_lat, tw_lat,
                           _filter_tables(DEC_SEQ, dft_lat[0], tw_lat, filt_args), ev_hy_bias[i])
            o_ctx = _gqa_attention(q, 0, SEQ, BATCH, SEQ, (k_ro, v_bf))
            cache = (cache_gqa_k[:, i].reshape(DEC_BATCH, PAST_LEN, LANES), cache_gqa_v[:, i].reshape(DEC_BATCH, PAST_LEN, LANES))
            o_lat = _gqa_attention(q, ctx_blocks_lat, DEC_SEQ, DEC_BATCH, 128, (k_ro, v_bf), cache)
            st_k.append(k_n[:T_CTX])
            st_v.append(v_32[:T_CTX])
            x = _out_proj([y_ctx, o_ctx], [y_lat, o_lat], [w_o_hy, w_o_att], x, mod)
        else:
            w_in, w_uq, w_uk, w_uv, qg, kg = _odd_weights(od_w_in[i], od_w_uq[i], od_w_ukv[i], od_q_norm[i], od_k_norm[i])
            q, k, v, ckv, kpe = _odd_in(x, norm_mix[l], mod, w_in, od_q_lora_norm[i].reshape(1, -1),
                                        od_kv_lora_norm[i].reshape(1, -1), w_uq, w_uk, w_uv, qg, kg, rope_mla)
            st_ckv.append(ckv[:T_CTX])
            st_kr.append(kpe[:T_CTX, MLA_NOPE_DIM:MLA_QK_DIM])
            cache_kpe = jnp.pad(cache_mla_krope[:, i].reshape(DEC_BATCH * PAST_LEN, MLA_ROPE_DIM),
                                ((0, 0), (MLA_NOPE_DIM, LANES - MLA_QK_DIM)))
            cache = _mla_cache(cache_mla_ckv[:, i].reshape(DEC_BATCH * PAST_LEN, MLA_KV_RANK), cache_kpe, w_uk, w_uv, kg)
            o_ctx = _mla_attention(q, k, v, 0, SEQ, BATCH, SEQ)
            o_lat = _mla_attention(q, k, v, ctx_blocks_lat, DEC_SEQ, DEC_BATCH, 512, cache)
            x = _out_proj([o_ctx], [o_lat], [od_w_out[i].astype(BF16)], x, mod)

        h, route = _router(x, norm_ffn[l], mod, rw_pm, rb_pm)
        y = _moe(h, *_route_tables(route), *_group_weights(moe_w_gate[l], moe_w_up[l], moe_w_down[l]))
        x = _residual(x, y, mod)

    y_prompt = _merge_parity(x[:T_CTX], BATCH, SEQ)
    y_sample = _merge_parity(x[T_CTX:], DEC_BATCH, DEC_SEQ)
    state = lambda parts, shape: jnp.stack([_merge_parity(p, BATCH, SEQ) for p in parts], axis=1).reshape(shape)
    return (y_prompt, y_sample,
            state(st_k, (BATCH, DEPTH // 2, SEQ, GQA_KV_HEADS, GQA_HEAD_DIM)),
            state(st_v, (BATCH, DEPTH // 2, SEQ, GQA_KV_HEADS, GQA_HEAD_DIM)),
            state(st_ckv, (BATCH, DEPTH // 2, SEQ, MLA_KV_RANK)),
            state(st_kr, (BATCH, DEPTH // 2, SEQ, MLA_ROPE_DIM)))
```

```python
import functools
import math

import jax
import jax.numpy as jnp
from jax import lax
from jax.experimental import pallas as pl
from jax.experimental.pallas import tpu as pltpu

F32 = jnp.float32
BF16 = jnp.bfloat16

D_MODEL = 1024
BATCH = 16
SEQ = 256
DEPTH = 4
DEC_BATCH = 8
DEC_SEQ = 2048
PAST_LEN = 512
GRID_W = 64
EPS = 1e-6
ROPE_THETA = 10000.0
N_MOD = 6
HY_DIM = D_MODEL // 2
HY_ORDER = 2
FILT_EMB = 33
FILT_BANDS = (FILT_EMB - 1) // 2
HY_MIN_DECAY = math.log(1e-2) / 1.5
HY_MAX_DECAY = math.log(1e-2) / 0.3
GQA_HEADS = 8
GQA_KV_HEADS = 2
GQA_HEAD_DIM = 64
GQA_GROUP = GQA_HEADS // GQA_KV_HEADS
GQA_Q = GQA_HEADS * GQA_HEAD_DIM
GQA_KV = GQA_KV_HEADS * GQA_HEAD_DIM
HY_IN = (HY_ORDER + 1) * HY_DIM
MLA_HEADS = 16
MLA_NOPE_DIM = 64
MLA_ROPE_DIM = 32
MLA_QK_DIM = MLA_NOPE_DIM + MLA_ROPE_DIM
MLA_V_DIM = 64
MLA_Q_RANK = 256
MLA_KV_RANK = 128
N_EXPERTS = 16
N_GROUPS = 4
EXPERTS_PER_GROUP = N_EXPERTS // N_GROUPS
D_EXPERT = 256

LANES = 128
T_CTX = BATCH * SEQ
T_LAT = DEC_BATCH * DEC_SEQ
T_ALL = T_CTX + T_LAT
N_COND = 1 + DEC_BATCH
COND_ROWS = 16
TM = 512
N_CTX_TILES = T_CTX // TM
TILES_PER_SEQ = DEC_SEQ // TM
GQA_SLOTS = GQA_HEADS * LANES
MLA_SLOTS = MLA_HEADS * LANES
EVEN_COLS = HY_IN + GQA_SLOTS + 2 * LANES
ODD_COLS = MLA_Q_RANK + MLA_KV_RANK + LANES
HY_CT = 128
KV_CHUNK = 512
LOG2E = math.log2(math.e)

VMEM_LIMIT_BYTES = 48 * 1024 * 1024


def _params(*sem):
    return pltpu.CompilerParams(dimension_semantics=sem, vmem_limit_bytes=VMEM_LIMIT_BYTES)


def _cond_index(i):
    return (jnp.where(i < N_CTX_TILES, 0, 1 + (i - N_CTX_TILES) // TILES_PER_SEQ), 0, 0)


def _rope_index(i):
    return (jnp.where(i < N_CTX_TILES, 0, 1 + (i - N_CTX_TILES) % TILES_PER_SEQ), 0)


def _ctx_index(i):
    return (jnp.minimum(i, N_CTX_TILES - 1), 0)


def _lat_index(i):
    return (jnp.maximum(i - N_CTX_TILES, 0), 0)


def _const2(i):
    return (0, 0)


def _norm_mod(x, gain, mod_ref, shift_row, scale_row):
    r = lax.rsqrt(jnp.mean(x * x, axis=-1, keepdims=True) + EPS)
    h = (x * r) * gain
    return h * (1.0 + mod_ref[scale_row:scale_row + 1, :]) + mod_ref[shift_row:shift_row + 1, :]


def _slot_norm(t, dims, gain):
    r = lax.rsqrt(jnp.sum(t * t, axis=-1, keepdims=True) / dims + EPS)
    return (t * r) * gain


def _rope(t, tabs, partner):
    cos, sin_up, sin_dn = tabs
    return t * cos + pltpu.roll(t, partner, 1) * sin_up + pltpu.roll(t, LANES - partner, 1) * sin_dn


def _mod_kernel(c_ref, w_ref, b_ref, o_ref):
    c = c_ref[...]
    a = (c * jax.nn.sigmoid(c)).astype(BF16)
    o_ref[...] = jnp.dot(a, w_ref[...].astype(BF16), preferred_element_type=F32) + b_ref[...]


def _modulation(cond, w_mod, b_mod):
    tn = 1536
    return pl.pallas_call(
        _mod_kernel,
        out_shape=jax.ShapeDtypeStruct((DEPTH, COND_ROWS, N_MOD * D_MODEL), F32),
        grid=(DEPTH, (N_MOD * D_MODEL) // tn),
        in_specs=[
            pl.BlockSpec((COND_ROWS, D_MODEL), lambda l, j: (0, 0)),
            pl.BlockSpec((None, D_MODEL, tn), lambda l, j: (l, 0, j)),
            pl.BlockSpec((None, 1, tn), lambda l, j: (l, 0, j)),
        ],
        out_specs=pl.BlockSpec((None, COND_ROWS, tn), lambda l, j: (l, 0, j)),
        compiler_params=_params("parallel", "parallel"),
        name="modulation",
    )(cond, w_mod, b_mod.reshape(DEPTH, 1, N_MOD * D_MODEL))


def _even_in_kernel(x_ref, g_ref, mod_ref, w_ref, qg_ref, kg_ref, rc_ref, ru_ref, rd_ref,
                    uhy_ref, q_ref, kro_ref, vbf_ref, kn_ref, v32_ref):
    h = _norm_mod(x_ref[...], g_ref[...], mod_ref, 0, 1)
    u = jnp.dot(h.astype(BF16), w_ref[...], preferred_element_type=F32)
    uhy_ref[...] = u[:, :HY_IN].astype(BF16)
    tabs = (rc_ref[...], ru_ref[...], rd_ref[...])
    partner = GQA_HEAD_DIM // 4
    qg = qg_ref[...]
    for s in range(GQA_HEADS):
        qs = _slot_norm(u[:, HY_IN + s * LANES:HY_IN + (s + 1) * LANES], GQA_HEAD_DIM, qg)
        q_ref[:, s * LANES:(s + 1) * LANES] = (_rope(qs, tabs, partner) * (GQA_HEAD_DIM ** -0.5 * LOG2E)).astype(BF16)
    k = u[:, HY_IN + GQA_SLOTS:HY_IN + GQA_SLOTS + LANES]
    lo = lax.broadcasted_iota(jnp.int32, k.shape, 1) < GQA_HEAD_DIM
    k2 = k * k
    ss_lo = jnp.sum(jnp.where(lo, k2, 0.0), axis=-1, keepdims=True)
    ss_hi = jnp.sum(jnp.where(lo, 0.0, k2), axis=-1, keepdims=True)
    r = jnp.where(lo, lax.rsqrt(ss_lo / GQA_HEAD_DIM + EPS), lax.rsqrt(ss_hi / GQA_HEAD_DIM + EPS))
    kn = (k * r) * kg_ref[...]
    kn_ref[...] = kn
    kro_ref[...] = _rope(kn, tabs, partner).astype(BF16)
    v = u[:, HY_IN + GQA_SLOTS + LANES:]
    v32_ref[...] = v
    vbf_ref[...] = v.astype(BF16)


def _even_in(x, gain, mod, w, qg, kg, tabs):
    tok = lambda n, dt: jax.ShapeDtypeStruct((T_ALL, n), dt)
    row = lambda n: pl.BlockSpec((TM, n), lambda i: (i, 0))
    tab = pl.BlockSpec((TM, LANES), _rope_index)
    return pl.pallas_call(
        _even_in_kernel,
        out_shape=(tok(HY_IN, BF16), tok(GQA_SLOTS, BF16), tok(LANES, BF16), tok(LANES, BF16),
                   tok(LANES, F32), tok(LANES, F32)),
        grid=(T_ALL // TM,),
        in_specs=[row(D_MODEL), pl.BlockSpec((1, D_MODEL), _const2), pl.BlockSpec((None, N_MOD, D_MODEL), _cond_index),
                  pl.BlockSpec((D_MODEL, EVEN_COLS), _const2), pl.BlockSpec((1, LANES), _const2),
                  pl.BlockSpec((1, LANES), _const2), tab, tab, tab],
        out_specs=(row(HY_IN), row(GQA_SLOTS), row(LANES), row(LANES), row(LANES), row(LANES)),
        compiler_params=_params("parallel"),
        name="even_in",
    )(x, gain.reshape(1, D_MODEL), mod, w, qg, kg, *tabs)


def _mla_heads(qs_all, ks_all, kpe, qg, kg, tabs, q_ref, k_ref):
    partner = MLA_ROPE_DIM // 4
    for s in range(MLA_HEADS):
        sl = slice(s * LANES, (s + 1) * LANES)
        if qs_all is not None:
            qs = _slot_norm(qs_all[:, sl], MLA_QK_DIM, qg)
            q_ref[:, sl] = (_rope(qs, tabs, partner) * (MLA_QK_DIM ** -0.5 * LOG2E)).astype(BF16)
        ks = _slot_norm(ks_all[:, sl] + kpe, MLA_QK_DIM, kg)
        k_ref[:, sl] = (ks if tabs is None else _rope(ks, tabs, partner)).astype(BF16)


def _odd_in_kernel(x_ref, g_ref, mod_ref, win_ref, qln_ref, kvln_ref, wuq_ref, wuk_ref, wuv_ref, qg_ref, kg_ref,
                   rc_ref, ru_ref, rd_ref, q_ref, k_ref, v_ref, ckv_ref, kpe_ref):
    h = _norm_mod(x_ref[...], g_ref[...], mod_ref, 0, 1)
    u = jnp.dot(h.astype(BF16), win_ref[...], preferred_element_type=F32)
    cq = u[:, :MLA_Q_RANK]
    cq = (cq * lax.rsqrt(jnp.mean(cq * cq, axis=-1, keepdims=True) + EPS)) * qln_ref[...]
    ckv = u[:, MLA_Q_RANK:MLA_Q_RANK + MLA_KV_RANK]
    ckv = (ckv * lax.rsqrt(jnp.mean(ckv * ckv, axis=-1, keepdims=True) + EPS)) * kvln_ref[...]
    kpe = u[:, MLA_Q_RANK + MLA_KV_RANK:]
    ckv_ref[...] = ckv
    kpe_ref[...] = kpe
    ckv_b = ckv.astype(BF16)
    v_ref[...] = jnp.dot(ckv_b, wuv_ref[...], preferred_element_type=F32).astype(BF16)
    qs_all = jnp.dot(cq.astype(BF16), wuq_ref[...], preferred_element_type=F32)
    ks_all = jnp.dot(ckv_b, wuk_ref[...], preferred_element_type=F32)
    _mla_heads(qs_all, ks_all, kpe, qg_ref[...], kg_ref[...], (rc_ref[...], ru_ref[...], rd_ref[...]), q_ref, k_ref)


def _odd_in(x, gain, mod, win, qln, kvln, wuq, wuk, wuv, qg, kg, tabs):
    tok = lambda n, dt: jax.ShapeDtypeStruct((T_ALL, n), dt)
    row = lambda n: pl.BlockSpec((TM, n), lambda i: (i, 0))
    full = lambda a: pl.BlockSpec(a.shape, _const2)
    tab = pl.BlockSpec((TM, LANES), _rope_index)
    return pl.pallas_call(
        _odd_in_kernel,
        out_shape=(tok(MLA_SLOTS, BF16), tok(MLA_SLOTS, BF16), tok(MLA_HEADS * MLA_V_DIM, BF16),
                   tok(LANES, F32), tok(LANES, F32)),
        grid=(T_ALL // TM,),
        in_specs=[row(D_MODEL), pl.BlockSpec((1, D_MODEL), _const2), pl.BlockSpec((None, N_MOD, D_MODEL), _cond_index),
                  full(win), full(qln), full(kvln), full(wuq), full(wuk), full(wuv), full(qg), full(kg), tab, tab, tab],
        out_specs=(row(MLA_SLOTS), row(MLA_SLOTS), row(MLA_HEADS * MLA_V_DIM), row(LANES), row(LANES)),
        compiler_params=_params("parallel"),
        name="odd_in",
    )(x, gain.reshape(1, D_MODEL), mod, win, qln, kvln, wuq, wuk, wuv, qg, kg, *tabs)


def _mla_cache_kernel(ckv_ref, kpe_ref, wuk_ref, wuv_ref, kg_ref, k_ref, v_ref):
    ckv_b = ckv_ref[...].astype(BF16)
    v_ref[...] = jnp.dot(ckv_b, wuv_ref[...], preferred_element_type=F32).astype(BF16)
    ks_all = jnp.dot(ckv_b, wuk_ref[...], preferred_element_type=F32)
    _mla_heads(None, ks_all, kpe_ref[...], None, kg_ref[...], None, None, k_ref)


def _mla_cache(ckv, kpe, wuk, wuv, kg):
    rows = ckv.shape[0]
    row = lambda n: pl.BlockSpec((TM, n), lambda i: (i, 0))
    full = lambda a: pl.BlockSpec(a.shape, _const2)
    return pl.pallas_call(
        _mla_cache_kernel,
        out_shape=(jax.ShapeDtypeStruct((rows, MLA_SLOTS), BF16), jax.ShapeDtypeStruct((rows, MLA_HEADS * MLA_V_DIM), BF16)),
        grid=(rows // TM,),
        in_specs=[row(LANES), row(LANES), full(wuk), full(wuv), full(kg)],
        out_specs=(row(MLA_SLOTS), row(MLA_HEADS * MLA_V_DIM)),
        compiler_params=_params("parallel"),
        name="mla_cache",
    )(ckv, kpe, wuk, wuv, kg)


def _out_kernel(*refs, n_parts):
    ctx = refs[:n_parts]
    lat = refs[n_parts:2 * n_parts]
    ws = refs[2 * n_parts:3 * n_parts]
    x_ref, mod_ref, o_ref = refs[3 * n_parts:]
    i = pl.program_id(0)

    def emit(parts):
        y = jnp.dot(parts[0][...], ws[0][...], preferred_element_type=F32)
        for p, w in zip(parts[1:], ws[1:]):
            y += jnp.dot(p[...], w[...], preferred_element_type=F32)
        o_ref[...] = x_ref[...] + mod_ref[2:3, :] * y

    @pl.when(i < N_CTX_TILES)
    def _():
        emit(ctx)

    @pl.when(i >= N_CTX_TILES)
    def _():
        emit(lat)


def _out_proj(parts_ctx, parts_lat, ws, x, mod):
    n_parts = len(ws)
    specs = [pl.BlockSpec((TM, p.shape[1]), _ctx_index) for p in parts_ctx]
    specs += [pl.BlockSpec((TM, p.shape[1]), _lat_index) for p in parts_lat]
    specs += [pl.BlockSpec(w.shape, _const2) for w in ws]
    specs += [pl.BlockSpec((TM, D_MODEL), lambda i: (i, 0)), pl.BlockSpec((None, N_MOD, D_MODEL), _cond_index)]
    return pl.pallas_call(
        functools.partial(_out_kernel, n_parts=n_parts),
        out_shape=jax.ShapeDtypeStruct((T_ALL, D_MODEL), F32),
        grid=(T_ALL // TM,),
        in_specs=specs,
        out_specs=pl.BlockSpec((TM, D_MODEL), lambda i: (i, 0)),
        compiler_params=_params("parallel"),
        name="out_proj",
    )(*parts_ctx, *parts_lat, *ws, x, mod)


def _mm_kernel(a_ref, x_ref, o_ref):
    o_ref[...] = jnp.dot(a_ref[...], x_ref[...], preferred_element_type=F32)


def _matmul(a, x, tn=1024):
    m, k = a.shape
    n = x.shape[1]
    return pl.pallas_call(
        _mm_kernel,
        out_shape=jax.ShapeDtypeStruct((m, n), F32),
        grid=(n // tn,),
        in_specs=[pl.BlockSpec((m, k), _const2), pl.BlockSpec((k, tn), lambda j: (0, j))],
        out_specs=pl.BlockSpec((m, tn), lambda j: (0, j)),
        compiler_params=_params("parallel"),
        name="taps_dft",
    )(a, x)


def _hyena_kernel(v_ref, x1_ref, x2_ref, cw_ref, cb_ref, f_ref, g_ref, tc_ref, ts_ref, h_ref, bias_ref, o_ref):
    n, ct = v_ref.shape
    half = n // 2
    row0 = lax.broadcasted_iota(jnp.int32, (half, ct), 0) == 0
    row_last = lax.broadcasted_iota(jnp.int32, (half, ct), 0) == half - 1
    cos, sin = tc_ref[...], ts_ref[...]

    def conv3(ref, grp):
        s = ref[...].astype(F32)
        se, so = s[:half], s[half:]
        w0, w1, w2 = cw_ref[0, grp:grp + 1, :], cw_ref[1, grp:grp + 1, :], cw_ref[2, grp:grp + 1, :]
        b = cb_ref[grp:grp + 1, :]
        so_prev = jnp.where(row0, 0.0, pltpu.roll(so, 1, 0))
        se_next = jnp.where(row_last, 0.0, pltpu.roll(se, half - 1, 0))
        return (w0 * so_prev + w1 * se + w2 * so + b, w0 * se + w1 * so + w2 * se_next + b)

    def long_conv(ze, zo, order):
        eo = jnp.dot(f_ref[...], jnp.concatenate([ze, zo], axis=1).astype(BF16), preferred_element_type=F32)
        er, ei, orr, oi = eo[:half, :ct], eo[half:, :ct], eo[:half, ct:], eo[half:, ct:]
        har, hai, hbr, hbi = h_ref[order, 0], h_ref[order, 1], h_ref[order, 2], h_ref[order, 3]
        tr = cos * orr + sin * oi
        ti = cos * oi - sin * orr
        pr, pi_, mr, mi = er + tr, ei + ti, er - tr, ei - ti
        ypr = pr * har - pi_ * hai
        ypi = pr * hai + pi_ * har
        ymr = mr * hbr - mi * hbi
        ymi = mr * hbi + mi * hbr
        ar, ai = ypr + ymr, ypi + ymi
        dr, di = ypr - ymr, ypi - ymi
        br = cos * dr - sin * di
        bi = cos * di + sin * dr
        e0, o0, en, on = er[0:1], orr[0:1], ei[0:1], oi[0:1]
        p0 = (e0 + o0) * har[0:1]
        m0 = (e0 - o0) * hbr[0:1]
        ar = jnp.where(row0, p0 + m0, ar)
        br = jnp.where(row0, p0 - m0, br)
        ai = jnp.where(row0, 2.0 * (en * hai[0:1] + on * hbi[0:1]), ai)
        bi = jnp.where(row0, -2.0 * (en * hbi[0:1] - on * hai[0:1]), bi)
        ab = jnp.concatenate([jnp.concatenate([ar, ai], axis=0), jnp.concatenate([br, bi], axis=0)], axis=1)
        y = jnp.dot(g_ref[...], ab.astype(BF16), preferred_element_type=F32)
        bias = bias_ref[order:order + 1, :]
        return y[:, :ct] + bias * ze, y[:, ct:] + bias * zo

    ve, vo = conv3(v_ref, 0)
    x1e, x1o = conv3(x1_ref, 1)
    x2e, x2o = conv3(x2_ref, 2)
    c1e, c1o = long_conv(ve, vo, 0)
    z1e, z1o = x1e * c1e, x1o * c1o
    c2e, c2o = long_conv(z1e, z1o, 1)
    o_ref[:half, :] = (x2e * c2e).astype(o_ref.dtype)
    o_ref[half:, :] = (x2o * c2o).astype(o_ref.dtype)


def _hyena(u_hy, row_block0, n, batch, conv_w, conv_b, dft, twiddle, tables, bias):
    fwd, inv = dft
    half = n // 2
    nj = HY_DIM // HY_CT
    seq = lambda grp: pl.BlockSpec((n, HY_CT), lambda b, j: (row_block0 + b, grp * nj + j))
    return pl.pallas_call(
        _hyena_kernel,
        out_shape=jax.ShapeDtypeStruct((batch * n, HY_DIM), BF16),
        grid=(batch, nj),
        in_specs=[seq(0), seq(1), seq(2),
                  pl.BlockSpec((3, HY_ORDER + 1, HY_CT), lambda b, j: (0, 0, j)),
                  pl.BlockSpec((HY_ORDER + 1, HY_CT), lambda b, j: (0, j)),
                  pl.BlockSpec((n, half), lambda b, j: (0, 0)),
                  pl.BlockSpec((half, n), lambda b, j: (0, 0)),
                  pl.BlockSpec((half, HY_CT), lambda b, j: (0, 0)),
                  pl.BlockSpec((half, HY_CT), lambda b, j: (0, 0)),
                  pl.BlockSpec((HY_ORDER, 4, half, HY_CT), lambda b, j: (0, 0, 0, j)),
                  pl.BlockSpec((HY_ORDER, HY_CT), lambda b, j: (0, j))],
        out_specs=pl.BlockSpec((n, HY_CT), lambda b, j: (b, j)),
        compiler_params=_params("parallel", "parallel"),
        name="hyena",
    )(u_hy, u_hy, u_hy, conv_w.reshape(3, HY_ORDER + 1, HY_DIM), conv_b.reshape(HY_ORDER + 1, HY_DIM),
      fwd, inv, twiddle[0], twiddle[1], tables, bias)


def _attend(q, chunks, vhalf):
    is_value = (lax.broadcasted_iota(jnp.int32, (1, LANES), 1) < LANES // 2) == (vhalf == 0)
    m = acc = None
    for load_k, load_v in chunks:
        s = lax.dot_general(q, load_k(), (((1,), (1,)), ((), ())), preferred_element_type=F32)
        v1 = jnp.where(is_value, load_v(), jnp.ones((), BF16))
        mc = jnp.max(s, axis=-1, keepdims=True)
        if m is None:
            m_new = mc
            acc = jnp.dot(jnp.exp2(s - m_new).astype(BF16), v1, preferred_element_type=F32)
        else:
            m_new = jnp.maximum(m, mc)
            acc = jnp.exp2(m - m_new) * acc + jnp.dot(jnp.exp2(s - m_new).astype(BF16), v1, preferred_element_type=F32)
        m = m_new
    denom = pltpu.roll(acc, LANES // 2, 1)
    return jnp.where(is_value, acc / denom, 0.0)


def _chunks(refs_kv, col):
    out = []
    for k_ref, v_ref in refs_kv:
        for c in range(k_ref.shape[0] // min(KV_CHUNK, k_ref.shape[0])):
            tk = min(KV_CHUNK, k_ref.shape[0])
            out.append((lambda k_ref=k_ref, c=c, tk=tk: k_ref[c * tk:(c + 1) * tk, col * LANES:(col + 1) * LANES].astype(BF16),
                        lambda v_ref=v_ref, c=c, tk=tk: v_ref[c * tk:(c + 1) * tk, :].astype(BF16)))
    return out


def _gqa_kernel(*refs):
    q_ref, o_ref = refs[0], refs[-1]
    kv = [(refs[i], refs[i + 1]) for i in range(1, len(refs) - 1, 2)]
    tq = q_ref.shape[0]
    for g in range(GQA_KV_HEADS):
        q = jnp.concatenate([q_ref[:, (g * GQA_GROUP + j) * LANES:(g * GQA_GROUP + j + 1) * LANES]
                             for j in range(GQA_GROUP)], axis=0)
        o = _attend(q, _chunks(kv, 0), g)
        for j in range(GQA_GROUP):
            s = g * GQA_GROUP + j
            o_ref[:, s * LANES:(s + 1) * LANES] = o[j * tq:(j + 1) * tq].astype(o_ref.dtype)


def _gqa_attention(q, row_block0, n, batch, tq, kv_new, kv_cache=None):
    nq = n // tq
    specs = [pl.BlockSpec((tq, GQA_SLOTS), lambda b, i: (row_block0 * nq + b * nq + i, 0))]
    args = [q]
    if kv_cache is not None:
        specs += [pl.BlockSpec((None, PAST_LEN, LANES), lambda b, i: (b, 0, 0))] * 2
        args += list(kv_cache)
    specs += [pl.BlockSpec((n, LANES), lambda b, i: (row_block0 + b, 0))] * 2
    args += list(kv_new)
    return pl.pallas_call(
        _gqa_kernel,
        out_shape=jax.ShapeDtypeStruct((batch * n, GQA_SLOTS), BF16),
        grid=(batch, nq),
        in_specs=specs,
        out_specs=pl.BlockSpec((tq, GQA_SLOTS), lambda b, i: (b * nq + i, 0)),
        compiler_params=_params("parallel", "parallel"),
        name="gqa_attention",
    )(*args)


def _mla_kernel(*refs):
    q_ref, o_ref = refs[0], refs[-1]
    kv = [(refs[i], refs[i + 1]) for i in range(1, len(refs) - 1, 2)]
    o_ref[...] = (_attend(q_ref[:, :LANES], _chunks(kv, 0), 0)
                  + _attend(q_ref[:, LANES:], _chunks(kv, 1), 1)).astype(o_ref.dtype)


def _mla_attention(q, k_new, v_new, row_block0, n, batch, tq, cache=None):
    nq = n // tq
    pairs = MLA_HEADS // 2
    specs = [pl.BlockSpec((tq, 2 * LANES), lambda b, j, i: (row_block0 * nq + b * nq + i, j))]
    args = [q]
    if cache is not None:
        specs += [pl.BlockSpec((PAST_LEN, 2 * LANES), lambda b, j, i: (b, j)),
                  pl.BlockSpec((PAST_LEN, LANES), lambda b, j, i: (b, j))]
        args += list(cache)
    specs += [pl.BlockSpec((n, 2 * LANES), lambda b, j, i: (row_block0 + b, j)),
              pl.BlockSpec((n, LANES), lambda b, j, i: (row_block0 + b, j))]
    args += [k_new, v_new]
    return pl.pallas_call(
        _mla_kernel,
        out_shape=jax.ShapeDtypeStruct((batch * n, MLA_HEADS * MLA_V_DIM), BF16),
        grid=(batch, pairs, nq),
        in_specs=specs,
        out_specs=pl.BlockSpec((tq, LANES), lambda b, j, i: (b * nq + i, j)),
        compiler_params=_params("parallel", "parallel", "parallel"),
        name="mla_attention",
    )(*args)


def _router_kernel(x_ref, g_ref, mod_ref, rw_ref, rb_ref, h_ref, route_ref):
    h = _norm_mod(x_ref[...], g_ref[...], mod_ref, 3, 4)
    for cc in range(D_MODEL // LANES):
        h_ref[:, cc, :] = h[:, cc * LANES:(cc + 1) * LANES]
    logits = lax.dot_general(rw_ref[...], h, (((1,), (1,)), ((), ())),
                             precision=lax.Precision.HIGHEST, preferred_element_type=F32)
    score = jax.nn.sigmoid(logits)
    sel = score + rb_ref[...]
    ng = N_GROUPS
    sv = [sel[p * ng:(p + 1) * ng, :] for p in range(EXPERTS_PER_GROUP)]
    hi01, lo01 = jnp.maximum(sv[0], sv[1]), jnp.minimum(sv[0], sv[1])
    hi23, lo23 = jnp.maximum(sv[2], sv[3]), jnp.minimum(sv[2], sv[3])
    top1 = jnp.maximum(hi01, hi23)
    top2 = jnp.maximum(jnp.minimum(hi01, hi23), jnp.maximum(lo01, lo23))
    gscore = top1 + top2
    gidx = lax.broadcasted_iota(jnp.int32, gscore.shape, 0)
    beaten = jnp.zeros(gscore.shape, jnp.int32)
    for j in range(ng):
        other = gscore[j:j + 1, :]
        beaten += ((other > gscore) | ((other == gscore) & (j < gidx))).astype(jnp.int32)
    best = beaten == 0
    chosen = []
    for p in range(EXPERTS_PER_GROUP):
        rank = jnp.zeros(gscore.shape, jnp.int32)
        for pp in range(EXPERTS_PER_GROUP):
            if pp != p:
                wins = (sv[pp] > sv[p]) | ((sv[pp] == sv[p]) & (pp < p))
                rank += wins.astype(jnp.int32)
        chosen.append((rank < 2) & best)
    picked = [jnp.where(chosen[p], score[p * ng:(p + 1) * ng, :], 0.0) for p in range(EXPERTS_PER_GROUP)]
    total = jnp.sum(picked[0] + picked[1] + picked[2] + picked[3], axis=0, keepdims=True)
    rows = [jnp.sum(picked[p], axis=0, keepdims=True) / total for p in range(EXPERTS_PER_GROUP)]
    rows.append(jnp.sum(jnp.where(best, gidx, 0), axis=0, keepdims=True).astype(F32))
    rows.append(jnp.zeros((ROUTE_ROWS - EXPERTS_PER_GROUP - 1, total.shape[1]), F32))
    route_ref[...] = jnp.concatenate(rows, axis=0)


def _router(x, gain, mod, rw_pm, rb_pm):
    return pl.pallas_call(
        _router_kernel,
        out_shape=(jax.ShapeDtypeStruct((T_ALL, D_MODEL // LANES, LANES), F32),
                   jax.ShapeDtypeStruct((ROUTE_ROWS, T_ALL), F32)),
        grid=(T_ALL // TM,),
        in_specs=[
            pl.BlockSpec((TM, D_MODEL), lambda i: (i, 0)),
            pl.BlockSpec((1, D_MODEL), _const2),
            pl.BlockSpec((None, N_MOD, D_MODEL), _cond_index),
            pl.BlockSpec((N_EXPERTS, D_MODEL), _const2),
            pl.BlockSpec((N_EXPERTS, 1), _const2),
        ],
        out_specs=(pl.BlockSpec((TM, D_MODEL // LANES, LANES), lambda i: (i, 0, 0)),
                   pl.BlockSpec((ROUTE_ROWS, TM), lambda i: (0, i))),
        compiler_params=_params("parallel"),
        name="router",
    )(x, gain.reshape(1, D_MODEL), mod, rw_pm, rb_pm)


MOE_TG = 512
MOE_NT = T_ALL // MOE_TG + N_GROUPS
GROUP_FF = EXPERTS_PER_GROUP * D_EXPERT
ROUTE_ROWS = 8


def _moe_kernel(tg_ref, src_ref, dst_ref, h_hbm, gate_ref, wg_ref, wu_ref, wd_ref, y_hbm, hbuf, ybuf, gsem, ssem):
    j = pl.program_id(0)
    slot = j % 2
    other = 1 - slot

    def gather(tile, s):
        for r in range(MOE_TG):
            pltpu.make_async_copy(h_hbm.at[src_ref[tile * MOE_TG + r]], hbuf.at[s, r], gsem.at[s]).start()

    def scatter(tile, s):
        for r in range(MOE_TG):
            pltpu.make_async_copy(ybuf.at[s, r], y_hbm.at[dst_ref[tile * MOE_TG + r]], ssem.at[s]).start()

    def wait_gather(s):
        pltpu.make_async_copy(h_hbm.at[pl.ds(0, MOE_TG)], hbuf.at[s], gsem.at[s]).wait()

    def wait_scatter(s):
        pltpu.make_async_copy(ybuf.at[s], y_hbm.at[pl.ds(0, MOE_TG)], ssem.at[s]).wait()

    @pl.when(j == 0)
    def _():
        gather(0, 0)
        ybuf[...] = jnp.zeros_like(ybuf)
        for s in range(2):
            spare = pltpu.make_async_copy(ybuf.at[s], y_hbm.at[pl.ds(T_ALL + s * MOE_TG, MOE_TG)], ssem.at[s])
            spare.start()
            spare.wait()

    gather(j + 1, other)
    wait_gather(slot)
    nc = D_MODEL // LANES
    h = jnp.concatenate([hbuf[slot, :, cc, :] for cc in range(nc)], axis=1).astype(BF16)
    a = jnp.dot(h, wg_ref[...], preferred_element_type=F32)
    u = jnp.dot(h, wu_ref[...], preferred_element_type=F32)
    gates = gate_ref[...]
    mid = (a * jax.nn.sigmoid(a)) * u
    mid = jnp.concatenate([mid[:, p * D_EXPERT:(p + 1) * D_EXPERT] * gates[:, p:p + 1]
                           for p in range(EXPERTS_PER_GROUP)], axis=1)
    y = jnp.dot(mid.astype(BF16), wd_ref[...], preferred_element_type=F32)

    @pl.when(j >= 2)
    def _():
        wait_scatter(slot)

    for cc in range(nc):
        ybuf[slot, :, cc, :] = y[:, cc * LANES:(cc + 1) * LANES]
    scatter(j, slot)

    @pl.when(j == MOE_NT - 1)
    def _():
        wait_gather(other)
        wait_scatter(other)
        wait_scatter(slot)


def _moe(h, tile_group, src_rows, dst_rows, gates_sorted, wg, wu, wd):
    wspec = lambda k, n: pl.BlockSpec((None, k, n), lambda j, tg, src, dst: (tg[j], 0, 0))
    return pl.pallas_call(
        _moe_kernel,
        out_shape=jax.ShapeDtypeStruct((T_ALL + 2 * MOE_TG, D_MODEL // LANES, LANES), F32),
        grid_spec=pltpu.PrefetchScalarGridSpec(
            num_scalar_prefetch=3,
            grid=(MOE_NT,),
            in_specs=[pl.BlockSpec(memory_space=pl.ANY),
                      pl.BlockSpec((MOE_TG, EXPERTS_PER_GROUP), lambda j, tg, src, dst: (j, 0)),
                      wspec(D_MODEL, GROUP_FF), wspec(D_MODEL, GROUP_FF), wspec(GROUP_FF, D_MODEL)],
            out_specs=pl.BlockSpec(memory_space=pl.ANY),
            scratch_shapes=[pltpu.VMEM((2, MOE_TG, D_MODEL // LANES, LANES), F32),
                            pltpu.VMEM((2, MOE_TG, D_MODEL // LANES, LANES), F32),
                            pltpu.SemaphoreType.DMA((2,)), pltpu.SemaphoreType.DMA((2,))]),
        compiler_params=_params("arbitrary"),
        name="experts",
    )(tile_group, src_rows, dst_rows, h, gates_sorted, wg, wu, wd)


def _residual_kernel(x_ref, y_ref, mod_ref, o_ref):
    y = jnp.concatenate([y_ref[:, cc, :] for cc in range(D_MODEL // LANES)], axis=1)
    o_ref[...] = x_ref[...] + mod_ref[5:6, :] * y


def _residual(x, y, mod):
    return pl.pallas_call(
        _residual_kernel,
        out_shape=jax.ShapeDtypeStruct((T_ALL, D_MODEL), F32),
        grid=(T_ALL // TM,),
        in_specs=[pl.BlockSpec((TM, D_MODEL), lambda i: (i, 0)),
                  pl.BlockSpec((TM, D_MODEL // LANES, LANES), lambda i: (i, 0, 0)),
                  pl.BlockSpec((None, N_MOD, D_MODEL), _cond_index)],
        out_specs=pl.BlockSpec((TM, D_MODEL), lambda i: (i, 0)),
        compiler_params=_params("parallel"),
        name="moe_residual",
    )(x, y, mod)


ROUTE_TABLE_ROWS = -(-(MOE_NT + 1) * MOE_TG // 1024) * 1024


def _invert_kernel(dest_ref, src_ref):
    def clear(p, carry):
        src_ref[p] = 0
        return carry

    def place(t, carry):
        src_ref[dest_ref[t]] = t
        return carry

    lax.fori_loop(0, ROUTE_TABLE_ROWS, clear, 0, unroll=8)
    lax.fori_loop(0, T_ALL, place, 0, unroll=8)


def _invert_rows(dest):
    return pl.pallas_call(
        _invert_kernel,
        out_shape=jax.ShapeDtypeStruct((ROUTE_TABLE_ROWS,), jnp.int32),
        in_specs=[pl.BlockSpec(memory_space=pltpu.SMEM)],
        out_specs=pl.BlockSpec(memory_space=pltpu.SMEM),
        name="invert_rows",
    )(dest)


def _route_tables(route):
    gid = route[EXPERTS_PER_GROUP].astype(jnp.int32)
    onehot = (gid[:, None] == jnp.arange(N_GROUPS, dtype=jnp.int32)[None, :]).astype(jnp.int32)
    count = jnp.sum(onehot, axis=0)
    rank = jnp.sum((jnp.cumsum(onehot, axis=0) - onehot) * onehot, axis=1)
    padded = ((count + MOE_TG - 1) // MOE_TG) * MOE_TG
    ends = jnp.cumsum(padded)
    starts = ends - padded
    src_rows = _invert_rows(starts[gid] + rank)
    rows = MOE_NT * MOE_TG
    tile_start = jnp.arange(MOE_NT, dtype=jnp.int32) * MOE_TG
    tile_group = jnp.minimum(jnp.sum((tile_start[:, None] >= ends[None, :]).astype(jnp.int32), axis=1), N_GROUPS - 1)
    r = jnp.arange(rows, dtype=jnp.int32)
    row_group = jnp.repeat(tile_group, MOE_TG)
    valid = (r - starts[row_group]) < count[row_group]
    spare = T_ALL + ((r // MOE_TG) % 2) * MOE_TG + r % MOE_TG
    dst_rows = jnp.where(valid, src_rows[:rows], spare)
    gates_sorted = jnp.where(valid[:, None], route[:EXPERTS_PER_GROUP].T[src_rows[:rows]], 0.0)
    return tile_group, src_rows, dst_rows, gates_sorted


def _group_weights(wg, wu, wd):
    wide = lambda w: (w.reshape(N_GROUPS, EXPERTS_PER_GROUP, D_MODEL, D_EXPERT).transpose(0, 2, 1, 3)
                      .reshape(N_GROUPS, D_MODEL, GROUP_FF).astype(BF16))
    return wide(wg), wide(wu), wd.reshape(N_GROUPS, GROUP_FF, D_MODEL).astype(BF16)


def _split_parity(a, batch, n):
    rest = a.shape[2:]
    return a.reshape(batch, n // 2, 2, *rest).swapaxes(1, 2).reshape(batch * n, *rest)


def _merge_parity(a, batch, n):
    rest = a.shape[1:]
    return a.reshape(batch, 2, n // 2, *rest).swapaxes(1, 2).reshape(batch, n, *rest)


def _dft_matrices(n):
    f = jnp.arange(n, dtype=jnp.int32)[:, None]
    t = jnp.arange(n, dtype=jnp.int32)[None, :]
    ang = ((f * t) % (2 * n)).astype(F32) * (math.pi / n)
    alt = jnp.where(t % 2 == 0, 1.0, -1.0).astype(F32)
    fwd = jnp.concatenate([jnp.cos(ang), alt, -jnp.sin(ang)[1:]], axis=0)
    row = jnp.arange(2 * n, dtype=jnp.int32)[:, None]
    weight = jnp.where((row == 0) | (row == n), 0.5 / n, 1.0 / n).astype(F32)
    inv = (fwd * weight).T
    return fwd, inv


def _conv_constants(n):
    half = n // 2
    fwd, inv = _dft_matrices(half)
    ang = jnp.arange(half, dtype=F32)[:, None] * (math.pi / n)
    cos = jnp.broadcast_to(jnp.cos(ang), (half, HY_CT))
    sin = jnp.broadcast_to(jnp.sin(ang), (half, HY_CT))
    return (fwd.astype(BF16), (0.5 * inv).astype(BF16)), (cos, sin)


def _filter_taps(n, w1, b1, w2, b2, w3, b3, freq):
    hp = lax.Precision.HIGHEST
    t01 = jnp.linspace(0.0, 1.0, n, dtype=F32)
    w = (2.0 * math.pi / n) * jnp.arange(n, dtype=F32)
    bands = jnp.linspace(1e-4, FILT_BANDS - 1, FILT_BANDS, dtype=F32)
    z = jnp.concatenate([t01[:, None], jnp.cos(w[:, None] * bands[None, :]),
                         -jnp.sin(w[:, None] * bands[None, :])], axis=-1)
    h = jnp.sin(freq * (jnp.dot(z, w1, precision=hp) + b1))
    h = jnp.sin(freq * (jnp.dot(h, w2, precision=hp) + b2))
    h = (jnp.dot(h, w3, precision=hp) + b3).reshape(n, HY_ORDER, 2, HY_DIM)
    deltas = jnp.abs(jnp.linspace(HY_MIN_DECAY, HY_MAX_DECAY, HY_DIM, dtype=F32))
    h = h * jnp.exp(-t01[:, None] * deltas[None, :])[:, None, None, :]
    h_fwd = h[:, :, 0]
    h_bwd = h[:, :, 1] * (jnp.arange(n) > 0)[:, None, None].astype(F32)
    scale = lax.rsqrt(jnp.sum(h_fwd * h_fwd, axis=0, keepdims=True) + jnp.sum(h_bwd * h_bwd, axis=0, keepdims=True) + EPS)
    return h_fwd * scale, h_bwd * scale


def _filter_tables(n, fwd, twiddle, filt_args):
    half = n // 2
    h_fwd, h_bwd = _filter_taps(n, *filt_args)
    cols = HY_ORDER * HY_DIM
    taps = jnp.concatenate([h_fwd.reshape(n, cols), h_bwd.reshape(n, cols)], axis=1)
    eo = _matmul(fwd, jnp.concatenate([taps[0::2], taps[1::2]], axis=1).astype(BF16))
    e, o = eo[:, :2 * cols], eo[:, 2 * cols:]
    er, ei, orr, oi = e[:half], e[half:], o[:half], o[half:]
    cos, sin = twiddle[0][:, :1], twiddle[1][:, :1]
    tr = cos * orr + sin * oi
    ti = cos * oi - sin * orr
    f_, b_ = slice(0, cols), slice(cols, 2 * cols)
    har = (er + tr)[:, f_] + (er + tr)[:, b_]
    hai = (ei + ti)[:, f_] - (ei + ti)[:, b_]
    hbr = (er - tr)[:, f_] + (er - tr)[:, b_]
    hbi = (ei - ti)[:, f_] - (ei - ti)[:, b_]
    first = (jnp.arange(half) == 0)[:, None]
    hai = jnp.where(first, ei[0:1, f_] + ei[0:1, b_], hai)
    hbi = jnp.where(first, oi[0:1, b_] - oi[0:1, f_], hbi)
    tabs = jnp.stack([har, hai, hbr, hbi], axis=0).reshape(4, half, HY_ORDER, HY_DIM)
    return tabs.transpose(2, 0, 1, 3)


def _positions(n):
    r = jnp.arange(n, dtype=jnp.int32)
    return jnp.where(r < n // 2, 2 * r, 2 * (r - n // 2) + 1)


def _rope_tables(first_lane, width, period):
    pos = _positions(DEC_SEQ)
    lane = jnp.arange(LANES, dtype=jnp.int32)
    d = lane % period - first_lane
    active = (d >= 0) & (d < width)
    axis_w = width // 2
    npair = axis_w // 2
    e = d % axis_w
    coord = jnp.where(d < axis_w, (pos // GRID_W)[:, None], (pos % GRID_W)[:, None]).astype(F32)
    freqs = ROPE_THETA ** (-(e % npair).astype(F32) / npair)
    ang = coord * freqs[None, :]
    cos = jnp.where(active[None, :], jnp.cos(ang), 1.0)
    sin = jnp.where(active[None, :], jnp.sin(ang), 0.0)
    second = (e >= npair)[None, :]
    sin_up = jnp.where(second, sin, 0.0)
    sin_dn = jnp.where(second, 0.0, -sin)
    ident = lambda v: jnp.full((TM, LANES), v, F32)
    return (jnp.concatenate([ident(1.0), cos], axis=0), jnp.concatenate([ident(0.0), sin_up], axis=0),
            jnp.concatenate([ident(0.0), sin_dn], axis=0))


def _kv_head_of_slot():
    return jax.nn.one_hot(jnp.arange(GQA_HEADS) // GQA_GROUP, GQA_KV_HEADS, dtype=F32)


def _even_weights(w_in, w_out):
    w_hy, w_q, w_k, w_v = jnp.split(w_in, [HY_IN, HY_IN + GQA_Q, HY_IN + GQA_Q + GQA_KV], axis=1)
    sel = _kv_head_of_slot()
    w_q = (w_q.reshape(D_MODEL, GQA_HEADS, 1, GQA_HEAD_DIM) * sel[None, :, :, None]).reshape(D_MODEL, GQA_SLOTS)
    w_in_p = jnp.concatenate([w_hy, w_q, w_k, w_v], axis=1).astype(BF16)
    w_o_hy, w_o_att = w_out[:HY_DIM], w_out[HY_DIM:]
    w_o_att = (w_o_att.reshape(GQA_HEADS, 1, GQA_HEAD_DIM, D_MODEL) * sel[:, :, None, None]).reshape(GQA_SLOTS, D_MODEL)
    return w_in_p, w_o_hy.astype(BF16), w_o_att.astype(BF16)


def _odd_weights(w_in, w_uq, w_ukv, q_norm, k_norm):
    pad = LANES - MLA_QK_DIM
    w_cq, w_ckv, w_kr = jnp.split(w_in, [MLA_Q_RANK, MLA_Q_RANK + MLA_KV_RANK], axis=1)
    w_kr = jnp.pad(w_kr, ((0, 0), (MLA_NOPE_DIM, pad)))
    w_in_p = jnp.concatenate([w_cq, w_ckv, w_kr], axis=1).astype(BF16)
    w_uq_p = jnp.pad(w_uq.reshape(MLA_Q_RANK, MLA_HEADS, MLA_QK_DIM), ((0, 0), (0, 0), (0, pad)))
    w_ukv = w_ukv.reshape(MLA_KV_RANK, MLA_HEADS, MLA_NOPE_DIM + MLA_V_DIM)
    w_uk_p = jnp.pad(w_ukv[:, :, :MLA_NOPE_DIM], ((0, 0), (0, 0), (0, LANES - MLA_NOPE_DIM)))
    w_uv = w_ukv[:, :, MLA_NOPE_DIM:]
    slot_gain = lambda g: jnp.pad(g, (0, pad)).reshape(1, LANES)
    return (w_in_p, w_uq_p.reshape(MLA_Q_RANK, MLA_SLOTS).astype(BF16), w_uk_p.reshape(MLA_KV_RANK, MLA_SLOTS).astype(BF16),
            w_uv.reshape(MLA_KV_RANK, MLA_HEADS * MLA_V_DIM).astype(BF16), slot_gain(q_norm), slot_gain(k_norm))


def kernel(x_prompt, x_sample, cache_gqa_k, cache_gqa_v, cache_mla_ckv, cache_mla_krope, c, c_ctx, norm_mix, norm_ffn, w_mod, b_mod, ev_w_in, ev_conv_w, ev_conv_b, ev_filt_w1, ev_filt_b1, ev_filt_w2, ev_filt_b2, ev_filt_w3, ev_filt_b3, ev_filt_freq, ev_hy_bias, ev_q_norm, ev_k_norm, ev_w_out, od_w_in, od_q_lora_norm, od_kv_lora_norm, od_w_uq, od_w_ukv, od_q_norm, od_k_norm, od_w_out, router_w, router_bias, moe_w_gate, moe_w_up, moe_w_down):
    x = jnp.concatenate([_split_parity(x_prompt, BATCH, SEQ), _split_parity(x_sample, DEC_BATCH, DEC_SEQ)], axis=0)
    cond = jnp.concatenate([c_ctx[None, :], c], axis=0)
    cond = jnp.pad(cond, ((0, COND_ROWS - N_COND), (0, 0)))
    mod_all = _modulation(cond, w_mod, b_mod).reshape(DEPTH, COND_ROWS, N_MOD, D_MODEL)

    perm = jnp.arange(N_EXPERTS).reshape(N_GROUPS, EXPERTS_PER_GROUP).T.reshape(-1)
    rw_pm = router_w.T[perm]
    rb_pm = router_bias[perm].reshape(N_EXPERTS, 1)

    dft_ctx, tw_ctx = _conv_constants(SEQ)
    dft_lat, tw_lat = _conv_constants(DEC_SEQ)
    rope_gqa = _rope_tables(0, GQA_HEAD_DIM, GQA_HEAD_DIM)
    rope_mla = _rope_tables(MLA_NOPE_DIM, MLA_ROPE_DIM, LANES)
    ctx_blocks_lat = T_CTX // DEC_SEQ

    st_k, st_v, st_ckv, st_kr = [], [], [], []
    for l in range(DEPTH):
        i = l // 2
        mod = mod_all[l]
        if l % 2 == 0:
            w_in, w_o_hy, w_o_att = _even_weights(ev_w_in[i], ev_w_out[i])
            both = lambda g: jnp.concatenate([g, g]).reshape(1, LANES)
            u_hy, q, k_ro, v_bf, k_n, v_32 = _even_in(x, norm_mix[l], mod, w_in, both(ev_q_norm[i]), both(ev_k_norm[i]),
                                                      rope_gqa)
            filt_args = (ev_filt_w1[i], ev_filt_b1[i], ev_filt_w2[i], ev_filt_b2[i], ev_filt_w3[i], ev_filt_b3[i],
                         ev_filt_freq[i])
            y_ctx = _hyena(u_hy, 0, SEQ, BATCH, ev_conv_w[i], ev_conv_b[i], dft_ctx, tw_ctx,
                           _filter_tables(SEQ, dft_ctx[0], tw_ctx, filt_args), ev_hy_bias[i])
            y_lat = _hyena(u_hy, ctx_blocks_lat, DEC_SEQ, DEC_BATCH, ev_conv_w[i], ev_conv_b[i], dft_lat, tw_lat,
                           _filter_tables(DEC_SEQ, dft_lat[0], tw_lat, filt_args), ev_hy_bias[i])
            o_ctx = _gqa_attention(q, 0, SEQ, BATCH, SEQ, (k_ro, v_bf))
            cache = (cache_gqa_k[:, i].reshape(DEC_BATCH, PAST_LEN, LANES), cache_gqa_v[:, i].reshape(DEC_BATCH, PAST_LEN, LANES))
            o_lat = _gqa_attention(q, ctx_blocks_lat, DEC_SEQ, DEC_BATCH, 128, (k_ro, v_bf), cache)
            st_k.append(k_n[:T_CTX])
            st_v.append(v_32[:T_CTX])
            x = _out_proj([y_ctx, o_ctx], [y_lat, o_lat], [w_o_hy, w_o_att], x, mod)
        else:
            w_in, w_uq, w_uk, w_uv, qg, kg = _odd_weights(od_w_in[i], od_w_uq[i], od_w_ukv[i], od_q_norm[i], od_k_norm[i])
            q, k, v, ckv, kpe = _odd_in(x, norm_mix[l], mod, w_in, od_q_lora_norm[i].reshape(1, -1),
                                        od_kv_lora_norm[i].reshape(1, -1), w_uq, w_uk, w_uv, qg, kg, rope_mla)
            st_ckv.append(ckv[:T_CTX])
            st_kr.append(kpe[:T_CTX, MLA_NOPE_DIM:MLA_QK_DIM])
            cache_kpe = jnp.pad(cache_mla_krope[:, i].reshape(DEC_BATCH * PAST_LEN, MLA_ROPE_DIM),
                                ((0, 0), (MLA_NOPE_DIM, LANES - MLA_QK_DIM)))
            cache = _mla_cache(cache_mla_ckv[:, i].reshape(DEC_BATCH * PAST_LEN, MLA_KV_RANK), cache_kpe, w_uk, w_uv, kg)
            o_ctx = _mla_attention(q, k, v, 0, SEQ, BATCH, SEQ)
            o_lat = _mla_attention(q, k, v, ctx_blocks_lat, DEC_SEQ, DEC_BATCH, 512, cache)
            x = _out_proj([o_ctx], [o_lat], [od_w_out[i].astype(BF16)], x, mod)

        h, route = _router(x, norm_ffn[l], mod, rw_pm, rb_pm)
        y = _moe(h, *_route_tables(route), *_group_weights(moe_w_gate[l], moe_w_up[l], moe_w_down[l]))
        x = _residual(x, y, mod)

    y_prompt = _merge_parity(x[:T_CTX], BATCH, SEQ)
    y_sample = _merge_parity(x[T_CTX:], DEC_BATCH, DEC_SEQ)
    state = lambda parts, shape: jnp.stack([_merge_parity(p, BATCH, SEQ) for p in parts], axis=1).reshape(shape)
    return (y_prompt, y_sample,
            state(st_k, (BATCH, DEPTH // 2, SEQ, GQA_KV_HEADS, GQA_HEAD_DIM)),
            state(st_v, (BATCH, DEPTH // 2, SEQ, GQA_KV_HEADS, GQA_HEAD_DIM)),
            state(st_ckv, (BATCH, DEPTH // 2, SEQ, MLA_KV_RANK)),
            state(st_kr, (BATCH, DEPTH // 2, SEQ, MLA_ROPE_DIM)))
```

```python
import functools
import math

import jax
import jax.numpy as jnp
from jax import lax
from jax.experimental import pallas as pl
from jax.experimental.pallas import tpu as pltpu

F32 = jnp.float32
BF16 = jnp.bfloat16

D_MODEL = 1024
BATCH = 16
SEQ = 256
DEPTH = 4
DEC_BATCH = 8
DEC_SEQ = 2048
PAST_LEN = 512
GRID_W = 64
EPS = 1e-6
ROPE_THETA = 10000.0
N_MOD = 6
HY_DIM = D_MODEL // 2
HY_ORDER = 2
FILT_EMB = 33
FILT_BANDS = (FILT_EMB - 1) // 2
HY_MIN_DECAY = math.log(1e-2) / 1.5
HY_MAX_DECAY = math.log(1e-2) / 0.3
GQA_HEADS = 8
GQA_KV_HEADS = 2
GQA_HEAD_DIM = 64
GQA_GROUP = GQA_HEADS // GQA_KV_HEADS
GQA_Q = GQA_HEADS * GQA_HEAD_DIM
GQA_KV = GQA_KV_HEADS * GQA_HEAD_DIM
HY_IN = (HY_ORDER + 1) * HY_DIM
MLA_HEADS = 16
MLA_NOPE_DIM = 64
MLA_ROPE_DIM = 32
MLA_QK_DIM = MLA_NOPE_DIM + MLA_ROPE_DIM
MLA_V_DIM = 64
MLA_Q_RANK = 256
MLA_KV_RANK = 128
N_EXPERTS = 16
N_GROUPS = 4
EXPERTS_PER_GROUP = N_EXPERTS // N_GROUPS
D_EXPERT = 256

LANES = 128
T_CTX = BATCH * SEQ
T_LAT = DEC_BATCH * DEC_SEQ
T_ALL = T_CTX + T_LAT
N_COND = 1 + DEC_BATCH
COND_ROWS = 16
TM = 512
N_CTX_TILES = T_CTX // TM
TILES_PER_SEQ = DEC_SEQ // TM
GQA_SLOTS = GQA_HEADS * LANES
MLA_SLOTS = MLA_HEADS * LANES
EVEN_COLS = HY_IN + GQA_SLOTS + 2 * LANES
HY_CT = 128
KV_CHUNK = 512
LOG2E = math.log2(math.e)

VMEM_LIMIT_BYTES = 48 * 1024 * 1024


def _params(*sem):
    return pltpu.CompilerParams(dimension_semantics=sem, vmem_limit_bytes=VMEM_LIMIT_BYTES)


def _cond_index(i):
    return (jnp.where(i < N_CTX_TILES, 0, 1 + (i - N_CTX_TILES) // TILES_PER_SEQ), 0, 0)


def _rope_index(i):
    return (jnp.where(i < N_CTX_TILES, 0, 1 + (i - N_CTX_TILES) % TILES_PER_SEQ), 0)


def _ctx_index(i):
    return (jnp.minimum(i, N_CTX_TILES - 1), 0)


def _lat_index(i):
    return (jnp.maximum(i - N_CTX_TILES, 0), 0)


def _const2(i):
    return (0, 0)


def _norm_mod(x, gain, mod_ref, shift_row, scale_row):
    r = lax.rsqrt(jnp.mean(x * x, axis=-1, keepdims=True) + EPS)
    h = (x * r) * gain
    return h * (1.0 + mod_ref[scale_row:scale_row + 1, :]) + mod_ref[shift_row:shift_row + 1, :]


def _slot_norm(t, dims, gain):
    r = lax.rsqrt(jnp.sum(t * t, axis=-1, keepdims=True) / dims + EPS)
    return (t * r) * gain


def _slot_rms(t, dims):
    return lax.rsqrt(jnp.sum(t * t, axis=-1, keepdims=True) / dims + EPS)


def _rope(t, tabs, partner):
    cos, sin_up, sin_dn = tabs
    return t * cos + pltpu.roll(t, partner, 1) * sin_up + pltpu.roll(t, LANES - partner, 1) * sin_dn


def _mod_kernel(c_ref, w_ref, b_ref, o_ref):
    c = c_ref[...]
    a = (c * jax.nn.sigmoid(c)).astype(BF16)
    o_ref[...] = jnp.dot(a, w_ref[...].astype(BF16), preferred_element_type=F32) + b_ref[...]


def _modulation(cond, w_mod, b_mod):
    tn = 1536
    return pl.pallas_call(
        _mod_kernel,
        out_shape=jax.ShapeDtypeStruct((DEPTH, COND_ROWS, N_MOD * D_MODEL), F32),
        grid=(DEPTH, (N_MOD * D_MODEL) // tn),
        in_specs=[
            pl.BlockSpec((COND_ROWS, D_MODEL), lambda l, j: (0, 0)),
            pl.BlockSpec((None, D_MODEL, tn), lambda l, j: (l, 0, j)),
            pl.BlockSpec((None, 1, tn), lambda l, j: (l, 0, j)),
        ],
        out_specs=pl.BlockSpec((None, COND_ROWS, tn), lambda l, j: (l, 0, j)),
        compiler_params=_params("parallel", "parallel"),
        name="modulation",
    )(cond, w_mod, b_mod.reshape(DEPTH, 1, N_MOD * D_MODEL))


def _even_in_kernel(x_ref, g_ref, mod_ref, w_ref, kg_ref, qc_ref, qu_ref, qd_ref, kc_ref, ku_ref, kd_ref,
                    uhy_ref, q_ref, kro_ref, vbf_ref, kn_ref, v32_ref):
    h = _norm_mod(x_ref[...], g_ref[...], mod_ref, 0, 1)
    u = jnp.dot(h.astype(BF16), w_ref[...], preferred_element_type=F32)
    uhy_ref[...] = u[:, :HY_IN].astype(BF16)
    qtabs = (qc_ref[...], qu_ref[...], qd_ref[...])
    partner = GQA_HEAD_DIM // 4
    for s in range(GQA_HEADS):
        qs = u[:, HY_IN + s * LANES:HY_IN + (s + 1) * LANES]
        q_ref[:, s * LANES:(s + 1) * LANES] = (_rope(qs, qtabs, partner) * _slot_rms(qs, GQA_HEAD_DIM)).astype(BF16)
    k = u[:, HY_IN + GQA_SLOTS:HY_IN + GQA_SLOTS + LANES]
    lo = lax.broadcasted_iota(jnp.int32, k.shape, 1) < GQA_HEAD_DIM
    k2 = k * k
    ss_lo = jnp.sum(jnp.where(lo, k2, 0.0), axis=-1, keepdims=True)
    ss_hi = jnp.sum(jnp.where(lo, 0.0, k2), axis=-1, keepdims=True)
    r = jnp.where(lo, lax.rsqrt(ss_lo / GQA_HEAD_DIM + EPS), lax.rsqrt(ss_hi / GQA_HEAD_DIM + EPS))
    kn_ref[...] = (k * r) * kg_ref[...]
    kro_ref[...] = (_rope(k, (kc_ref[...], ku_ref[...], kd_ref[...]), partner) * r).astype(BF16)
    v = u[:, HY_IN + GQA_SLOTS + LANES:]
    v32_ref[...] = v
    vbf_ref[...] = v.astype(BF16)


def _even_in(x, gain, mod, w, kg, qtabs, ktabs):
    tok = lambda n, dt: jax.ShapeDtypeStruct((T_ALL, n), dt)
    row = lambda n: pl.BlockSpec((TM, n), lambda i: (i, 0))
    tab = pl.BlockSpec((TM, LANES), _rope_index)
    return pl.pallas_call(
        _even_in_kernel,
        out_shape=(tok(HY_IN, BF16), tok(GQA_SLOTS, BF16), tok(LANES, BF16), tok(LANES, BF16),
                   tok(LANES, F32), tok(LANES, F32)),
        grid=(T_ALL // TM,),
        in_specs=[row(D_MODEL), pl.BlockSpec((1, D_MODEL), _const2), pl.BlockSpec((None, N_MOD, D_MODEL), _cond_index),
                  pl.BlockSpec((D_MODEL, EVEN_COLS), _const2), pl.BlockSpec((1, LANES), _const2),
                  tab, tab, tab, tab, tab, tab],
        out_specs=(row(HY_IN), row(GQA_SLOTS), row(LANES), row(LANES), row(LANES), row(LANES)),
        compiler_params=_params("parallel"),
        name="even_in",
    )(x, gain.reshape(1, D_MODEL), mod, w, kg, *qtabs, *ktabs)


def _odd_in_kernel(x_ref, g_ref, mod_ref, win_ref, qln_ref, kvln_ref, wuq_ref, wuqp_ref, wuk_ref, wuv_ref,
                   qc_ref, qs_ref, kc_ref, ks_ref, q_ref, k_ref, v_ref, ckv_ref, kpe_ref):
    h = _norm_mod(x_ref[...], g_ref[...], mod_ref, 0, 1)
    u = jnp.dot(h.astype(BF16), win_ref[...], preferred_element_type=F32)
    cq = u[:, :MLA_Q_RANK]
    cq = (cq * lax.rsqrt(jnp.mean(cq * cq, axis=-1, keepdims=True) + EPS)) * qln_ref[...]
    ckv = u[:, MLA_Q_RANK:MLA_Q_RANK + MLA_KV_RANK]
    ckv = (ckv * lax.rsqrt(jnp.mean(ckv * ckv, axis=-1, keepdims=True) + EPS)) * kvln_ref[...]
    kpe = u[:, MLA_Q_RANK + MLA_KV_RANK:MLA_Q_RANK + MLA_KV_RANK + LANES]
    kpe_partner = u[:, MLA_Q_RANK + MLA_KV_RANK + LANES:]
    ckv_ref[...] = ckv
    kpe_ref[...] = kpe
    ckv_b = ckv.astype(BF16)
    cq_b = cq.astype(BF16)
    v_ref[...] = jnp.dot(ckv_b, wuv_ref[...], preferred_element_type=F32).astype(BF16)
    q_all = jnp.dot(cq_b, wuq_ref[...], preferred_element_type=F32)
    q_partner = jnp.dot(cq_b, wuqp_ref[...], preferred_element_type=F32)
    k_all = jnp.dot(ckv_b, wuk_ref[...], preferred_element_type=F32)
    qcos, qsin, kcos, ksin = qc_ref[...], qs_ref[...], kc_ref[...], ks_ref[...]
    k_rot = kpe_partner * ksin
    for s in range(MLA_HEADS):
        sl = slice(s * LANES, (s + 1) * LANES)
        qs = q_all[:, sl]
        q_ref[:, sl] = ((qs * qcos + q_partner[:, sl] * qsin) * _slot_rms(qs, MLA_QK_DIM)).astype(BF16)
        ks = k_all[:, sl] + kpe
        k_ref[:, sl] = ((ks * kcos + k_rot) * _slot_rms(ks, MLA_QK_DIM)).astype(BF16)


def _odd_in(x, gain, mod, win, qln, kvln, wuq, wuqp, wuk, wuv, qtabs, ktabs):
    tok = lambda n, dt: jax.ShapeDtypeStruct((T_ALL, n), dt)
    row = lambda n: pl.BlockSpec((TM, n), lambda i: (i, 0))
    full = lambda a: pl.BlockSpec(a.shape, _const2)
    tab = pl.BlockSpec((TM, LANES), _rope_index)
    return pl.pallas_call(
        _odd_in_kernel,
        out_shape=(tok(MLA_SLOTS, BF16), tok(MLA_SLOTS, BF16), tok(MLA_HEADS * MLA_V_DIM, BF16),
                   tok(LANES, F32), tok(LANES, F32)),
        grid=(T_ALL // TM,),
        in_specs=[row(D_MODEL), pl.BlockSpec((1, D_MODEL), _const2), pl.BlockSpec((None, N_MOD, D_MODEL), _cond_index),
                  full(win), full(qln), full(kvln), full(wuq), full(wuqp), full(wuk), full(wuv), tab, tab, tab, tab],
        out_specs=(row(MLA_SLOTS), row(MLA_SLOTS), row(MLA_HEADS * MLA_V_DIM), row(LANES), row(LANES)),
        compiler_params=_params("parallel"),
        name="odd_in",
    )(x, gain.reshape(1, D_MODEL), mod, win, qln, kvln, wuq, wuqp, wuk, wuv, *qtabs, *ktabs)


def _mla_cache_kernel(ckv_ref, kpe_ref, wuk_ref, wuv_ref, kg_ref, k_ref, v_ref):
    ckv_b = ckv_ref[...].astype(BF16)
    v_ref[...] = jnp.dot(ckv_b, wuv_ref[...], preferred_element_type=F32).astype(BF16)
    k_all = jnp.dot(ckv_b, wuk_ref[...], preferred_element_type=F32)
    kpe, kg = kpe_ref[...], kg_ref[...]
    for s in range(MLA_HEADS):
        sl = slice(s * LANES, (s + 1) * LANES)
        k_ref[:, sl] = _slot_norm(k_all[:, sl] + kpe, MLA_QK_DIM, kg).astype(BF16)


def _mla_cache(ckv, kpe, wuk, wuv, kg):
    rows = ckv.shape[0]
    row = lambda n: pl.BlockSpec((TM, n), lambda i: (i, 0))
    full = lambda a: pl.BlockSpec(a.shape, _const2)
    return pl.pallas_call(
        _mla_cache_kernel,
        out_shape=(jax.ShapeDtypeStruct((rows, MLA_SLOTS), BF16), jax.ShapeDtypeStruct((rows, MLA_HEADS * MLA_V_DIM), BF16)),
        grid=(rows // TM,),
        in_specs=[row(LANES), row(LANES), full(wuk), full(wuv), full(kg)],
        out_specs=(row(MLA_SLOTS), row(MLA_HEADS * MLA_V_DIM)),
        compiler_params=_params("parallel"),
        name="mla_cache",
    )(ckv, kpe, wuk, wuv, kg)


def _out_kernel(*refs, n_parts):
    ctx = refs[:n_parts]
    lat = refs[n_parts:2 * n_parts]
    ws = refs[2 * n_parts:3 * n_parts]
    x_ref, mod_ref, o_ref = refs[3 * n_parts:]
    i = pl.program_id(0)

    def emit(parts):
        y = jnp.dot(parts[0][...], ws[0][...], preferred_element_type=F32)
        for p, w in zip(parts[1:], ws[1:]):
            y += jnp.dot(p[...], w[...], preferred_element_type=F32)
        o_ref[...] = x_ref[...] + mod_ref[2:3, :] * y

    @pl.when(i < N_CTX_TILES)
    def _():
        emit(ctx)

    @pl.when(i >= N_CTX_TILES)
    def _():
        emit(lat)


def _out_proj(parts_ctx, parts_lat, ws, x, mod):
    n_parts = len(ws)
    specs = [pl.BlockSpec((TM, p.shape[1]), _ctx_index) for p in parts_ctx]
    specs += [pl.BlockSpec((TM, p.shape[1]), _lat_index) for p in parts_lat]
    specs += [pl.BlockSpec(w.shape, _const2) for w in ws]
    specs += [pl.BlockSpec((TM, D_MODEL), lambda i: (i, 0)), pl.BlockSpec((None, N_MOD, D_MODEL), _cond_index)]
    return pl.pallas_call(
        functools.partial(_out_kernel, n_parts=n_parts),
        out_shape=jax.ShapeDtypeStruct((T_ALL, D_MODEL), F32),
        grid=(T_ALL // TM,),
        in_specs=specs,
        out_specs=pl.BlockSpec((TM, D_MODEL), lambda i: (i, 0)),
        compiler_params=_params("parallel"),
        name="out_proj",
    )(*parts_ctx, *parts_lat, *ws, x, mod)


def _mm_kernel(a_ref, x_ref, o_ref):
    o_ref[...] = jnp.dot(a_ref[...], x_ref[...], preferred_element_type=F32)


def _matmul(a, x, tn=1024):
    m, k = a.shape
    n = x.shape[1]
    return pl.pallas_call(
        _mm_kernel,
        out_shape=jax.ShapeDtypeStruct((m, n), F32),
        grid=(n // tn,),
        in_specs=[pl.BlockSpec((m, k), _const2), pl.BlockSpec((k, tn), lambda j: (0, j))],
        out_specs=pl.BlockSpec((m, tn), lambda j: (0, j)),
        compiler_params=_params("parallel"),
        name="taps_dft",
    )(a, x)


def _hyena_kernel(v_ref, x1_ref, x2_ref, cw_ref, cb_ref, f_ref, g_ref, tc_ref, ts_ref, h_ref, bias_ref, o_ref):
    n, ct = v_ref.shape
    half = n // 2
    row0 = lax.broadcasted_iota(jnp.int32, (half, ct), 0) == 0
    row_last = lax.broadcasted_iota(jnp.int32, (half, ct), 0) == half - 1
    cos, sin = tc_ref[...], ts_ref[...]

    def conv3(ref, grp):
        s = ref[...].astype(F32)
        se, so = s[:half], s[half:]
        w0, w1, w2 = cw_ref[0, grp:grp + 1, :], cw_ref[1, grp:grp + 1, :], cw_ref[2, grp:grp + 1, :]
        b = cb_ref[grp:grp + 1, :]
        so_prev = jnp.where(row0, 0.0, pltpu.roll(so, 1, 0))
        se_next = jnp.where(row_last, 0.0, pltpu.roll(se, half - 1, 0))
        return (w0 * so_prev + w1 * se + w2 * so + b, w0 * se + w1 * so + w2 * se_next + b)

    def long_conv(ze, zo, order):
        eo = jnp.dot(f_ref[...], jnp.concatenate([ze, zo], axis=1).astype(BF16), preferred_element_type=F32)
        er, ei, orr, oi = eo[:half, :ct], eo[half:, :ct], eo[:half, ct:], eo[half:, ct:]
        har, hai, hbr, hbi = h_ref[order, 0], h_ref[order, 1], h_ref[order, 2], h_ref[order, 3]
        tr = cos * orr + sin * oi
        ti = cos * oi - sin * orr
        pr, pi_, mr, mi = er + tr, ei + ti, er - tr, ei - ti
        ypr = pr * har - pi_ * hai
        ypi = pr * hai + pi_ * har
        ymr = mr * hbr - mi * hbi
        ymi = mr * hbi + mi * hbr
        ar, ai = ypr + ymr, ypi + ymi
        dr, di = ypr - ymr, ypi - ymi
        br = cos * dr - sin * di
        bi = cos * di + sin * dr
        e0, o0, en, on = er[0:1], orr[0:1], ei[0:1], oi[0:1]
        p0 = (e0 + o0) * har[0:1]
        m0 = (e0 - o0) * hbr[0:1]
        ar = jnp.where(row0, p0 + m0, ar)
        br = jnp.where(row0, p0 - m0, br)
        ai = jnp.where(row0, 2.0 * (en * hai[0:1] + on * hbi[0:1]), ai)
        bi = jnp.where(row0, -2.0 * (en * hbi[0:1] - on * hai[0:1]), bi)
        ab = jnp.concatenate([jnp.concatenate([ar, ai], axis=0), jnp.concatenate([br, bi], axis=0)], axis=1)
        y = jnp.dot(g_ref[...], ab.astype(BF16), preferred_element_type=F32)
        bias = bias_ref[order:order + 1, :]
        return y[:, :ct] + bias * ze, y[:, ct:] + bias * zo

    ve, vo = conv3(v_ref, 0)
    x1e, x1o = conv3(x1_ref, 1)
    x2e, x2o = conv3(x2_ref, 2)
    c1e, c1o = long_conv(ve, vo, 0)
    z1e, z1o = x1e * c1e, x1o * c1o
    c2e, c2o = long_conv(z1e, z1o, 1)
    o_ref[:half, :] = (x2e * c2e).astype(o_ref.dtype)
    o_ref[half:, :] = (x2o * c2o).astype(o_ref.dtype)


def _hyena(u_hy, row_block0, n, batch, conv_w, conv_b, dft, twiddle, tables, bias):
    fwd, inv = dft
    half = n // 2
    nj = HY_DIM // HY_CT
    seq = lambda grp: pl.BlockSpec((n, HY_CT), lambda b, j: (row_block0 + b, grp * nj + j))
    return pl.pallas_call(
        _hyena_kernel,
        out_shape=jax.ShapeDtypeStruct((batch * n, HY_DIM), BF16),
        grid=(batch, nj),
        in_specs=[seq(0), seq(1), seq(2),
                  pl.BlockSpec((3, HY_ORDER + 1, HY_CT), lambda b, j: (0, 0, j)),
                  pl.BlockSpec((HY_ORDER + 1, HY_CT), lambda b, j: (0, j)),
                  pl.BlockSpec((n, half), lambda b, j: (0, 0)),
                  pl.BlockSpec((half, n), lambda b, j: (0, 0)),
                  pl.BlockSpec((half, HY_CT), lambda b, j: (0, 0)),
                  pl.BlockSpec((half, HY_CT), lambda b, j: (0, 0)),
                  pl.BlockSpec((HY_ORDER, 4, half, HY_CT), lambda b, j: (0, 0, 0, j)),
                  pl.BlockSpec((HY_ORDER, HY_CT), lambda b, j: (0, j))],
        out_specs=pl.BlockSpec((n, HY_CT), lambda b, j: (b, j)),
        compiler_params=_params("parallel", "parallel"),
        name="hyena",
    )(u_hy, u_hy, u_hy, conv_w.reshape(3, HY_ORDER + 1, HY_DIM), conv_b.reshape(HY_ORDER + 1, HY_DIM),
      fwd, inv, twiddle[0], twiddle[1], tables, bias)


def _attend(q, chunks, vhalf):
    is_value = (lax.broadcasted_iota(jnp.int32, (1, LANES), 1) < LANES // 2) == (vhalf == 0)
    m = acc = None
    for load_k, load_v in chunks:
        s = lax.dot_general(q, load_k(), (((1,), (1,)), ((), ())), preferred_element_type=F32)
        v1 = jnp.where(is_value, load_v(), jnp.ones((), BF16))
        mc = jnp.max(s, axis=-1, keepdims=True)
        if m is None:
            m_new = mc
            acc = jnp.dot(jnp.exp2(s - m_new).astype(BF16), v1, preferred_element_type=F32)
        else:
            m_new = jnp.maximum(m, mc)
            acc = jnp.exp2(m - m_new) * acc + jnp.dot(jnp.exp2(s - m_new).astype(BF16), v1, preferred_element_type=F32)
        m = m_new
    denom = pltpu.roll(acc, LANES // 2, 1)
    return jnp.where(is_value, acc / denom, 0.0)


def _chunks(refs_kv, col):
    out = []
    for k_ref, v_ref in refs_kv:
        for c in range(k_ref.shape[0] // min(KV_CHUNK, k_ref.shape[0])):
            tk = min(KV_CHUNK, k_ref.shape[0])
            out.append((lambda k_ref=k_ref, c=c, tk=tk: k_ref[c * tk:(c + 1) * tk, col * LANES:(col + 1) * LANES].astype(BF16),
                        lambda v_ref=v_ref, c=c, tk=tk: v_ref[c * tk:(c + 1) * tk, :].astype(BF16)))
    return out


def _gqa_kernel(*refs):
    q_ref, o_ref = refs[0], refs[-1]
    kv = [(refs[i], refs[i + 1]) for i in range(1, len(refs) - 1, 2)]
    tq = q_ref.shape[0]
    for g in range(GQA_KV_HEADS):
        q = jnp.concatenate([q_ref[:, (g * GQA_GROUP + j) * LANES:(g * GQA_GROUP + j + 1) * LANES]
                             for j in range(GQA_GROUP)], axis=0)
        o = _attend(q, _chunks(kv, 0), g)
        for j in range(GQA_GROUP):
            s = g * GQA_GROUP + j
            o_ref[:, s * LANES:(s + 1) * LANES] = o[j * tq:(j + 1) * tq].astype(o_ref.dtype)


def _gqa_attention(q, row_block0, n, batch, tq, kv_new, kv_cache=None):
    nq = n // tq
    specs = [pl.BlockSpec((tq, GQA_SLOTS), lambda b, i: (row_block0 * nq + b * nq + i, 0))]
    args = [q]
    if kv_cache is not None:
        specs += [pl.BlockSpec((None, PAST_LEN, LANES), lambda b, i: (b, 0, 0))] * 2
        args += list(kv_cache)
    specs += [pl.BlockSpec((n, LANES), lambda b, i: (row_block0 + b, 0))] * 2
    args += list(kv_new)
    return pl.pallas_call(
        _gqa_kernel,
        out_shape=jax.ShapeDtypeStruct((batch * n, GQA_SLOTS), BF16),
        grid=(batch, nq),
        in_specs=specs,
        out_specs=pl.BlockSpec((tq, GQA_SLOTS), lambda b, i: (b * nq + i, 0)),
        compiler_params=_params("parallel", "parallel"),
        name="gqa_attention",
    )(*args)


def _mla_kernel(*refs):
    q_ref, o_ref = refs[0], refs[-1]
    kv = [(refs[i], refs[i + 1]) for i in range(1, len(refs) - 1, 2)]
    o_ref[...] = (_attend(q_ref[:, :LANES], _chunks(kv, 0), 0)
                  + _attend(q_ref[:, LANES:], _chunks(kv, 1), 1)).astype(o_ref.dtype)


def _mla_attention(q, k_new, v_new, row_block0, n, batch, tq, cache=None):
    nq = n // tq
    pairs = MLA_HEADS // 2
    specs = [pl.BlockSpec((tq, 2 * LANES), lambda b, j, i: (row_block0 * nq + b * nq + i, j))]
    args = [q]
    if cache is not None:
        specs += [pl.BlockSpec((PAST_LEN, 2 * LANES), lambda b, j, i: (b, j)),
                  pl.BlockSpec((PAST_LEN, LANES), lambda b, j, i: (b, j))]
        args += list(cache)
    specs += [pl.BlockSpec((n, 2 * LANES), lambda b, j, i: (row_block0 + b, j)),
              pl.BlockSpec((n, LANES), lambda b, j, i: (row_block0 + b, j))]
    args += [k_new, v_new]
    return pl.pallas_call(
        _mla_kernel,
        out_shape=jax.ShapeDtypeStruct((batch * n, MLA_HEADS * MLA_V_DIM), BF16),
        grid=(batch, pairs, nq),
        in_specs=specs,
        out_specs=pl.BlockSpec((tq, LANES), lambda b, j, i: (b * nq + i, j)),
        compiler_params=_params("parallel", "parallel", "parallel"),
        name="mla_attention",
    )(*args)


def _router_kernel(x_ref, g_ref, mod_ref, rw_ref, rb_ref, h_ref, gate_ref):
    h = _norm_mod(x_ref[...], g_ref[...], mod_ref, 3, 4)
    h_ref[...] = h.astype(BF16)
    logits = lax.dot_general(rw_ref[...], h, (((1,), (1,)), ((), ())),
                             precision=lax.Precision.HIGHEST, preferred_element_type=F32)
    score = jax.nn.sigmoid(logits)
    sel = score + rb_ref[...]
    ng = N_GROUPS
    sv = [sel[p * ng:(p + 1) * ng, :] for p in range(EXPERTS_PER_GROUP)]
    hi01, lo01 = jnp.maximum(sv[0], sv[1]), jnp.minimum(sv[0], sv[1])
    hi23, lo23 = jnp.maximum(sv[2], sv[3]), jnp.minimum(sv[2], sv[3])
    top1 = jnp.maximum(hi01, hi23)
    top2 = jnp.maximum(jnp.minimum(hi01, hi23), jnp.maximum(lo01, lo23))
    gscore = top1 + top2
    gidx = lax.broadcasted_iota(jnp.int32, gscore.shape, 0)
    beaten = jnp.zeros(gscore.shape, jnp.int32)
    for j in range(ng):
        other = gscore[j:j + 1, :]
        beaten += ((other > gscore) | ((other == gscore) & (j < gidx))).astype(jnp.int32)
    best = beaten == 0
    chosen = []
    for p in range(EXPERTS_PER_GROUP):
        rank = jnp.zeros(gscore.shape, jnp.int32)
        for pp in range(EXPERTS_PER_GROUP):
            if pp != p:
                wins = (sv[pp] > sv[p]) | ((sv[pp] == sv[p]) & (pp < p))
                rank += wins.astype(jnp.int32)
        chosen.append((rank < 2) & best)
    picked = [jnp.where(chosen[p], score[p * ng:(p + 1) * ng, :], 0.0) for p in range(EXPERTS_PER_GROUP)]
    total = jnp.sum(picked[0] + picked[1] + picked[2] + picked[3], axis=0, keepdims=True)
    gate_ref[...] = jnp.concatenate(picked, axis=0) / total


def _router(x, gain, mod, rw_pm, rb_pm):
    return pl.pallas_call(
        _router_kernel,
        out_shape=(jax.ShapeDtypeStruct((T_ALL, D_MODEL), BF16), jax.ShapeDtypeStruct((N_EXPERTS, T_ALL), F32)),
        grid=(T_ALL // TM,),
        in_specs=[
            pl.BlockSpec((TM, D_MODEL), lambda i: (i, 0)),
            pl.BlockSpec((1, D_MODEL), _const2),
            pl.BlockSpec((None, N_MOD, D_MODEL), _cond_index),
            pl.BlockSpec((N_EXPERTS, D_MODEL), _const2),
            pl.BlockSpec((N_EXPERTS, 1), _const2),
        ],
        out_specs=(pl.BlockSpec((TM, D_MODEL), lambda i: (i, 0)), pl.BlockSpec((N_EXPERTS, TM), lambda i: (0, i))),
        compiler_params=_params("parallel"),
        name="router",
    )(x, gain.reshape(1, D_MODEL), mod, rw_pm, rb_pm)


MOE_TM = 1024


def _moe_kernel(h_ref, gate_ref, wg_ref, wu_ref, wd_ref, x_ref, mod_ref, o_ref, acc_ref):
    e = pl.program_id(1)

    @pl.when(e == 0)
    def _():
        acc_ref[...] = jnp.zeros_like(acc_ref)

    h = h_ref[...]
    a = jnp.dot(h, wg_ref[...].astype(BF16), preferred_element_type=F32)
    u = jnp.dot(h, wu_ref[...].astype(BF16), preferred_element_type=F32)
    gates = gate_ref[...]
    lane = lax.broadcasted_iota(jnp.int32, gates.shape, 1)
    g = jnp.sum(jnp.where(lane == e, gates, 0.0), axis=1, keepdims=True)
    mid = (a * jax.nn.sigmoid(a)) * u * g
    acc_ref[...] += jnp.dot(mid.astype(BF16), wd_ref[...].astype(BF16), preferred_element_type=F32)

    @pl.when(e == N_EXPERTS - 1)
    def _():
        o_ref[...] = x_ref[...] + mod_ref[5:6, :] * acc_ref[...]


def _moe(h, gates, wg, wu, wd, x, mod):
    tm = MOE_TM
    n_ctx = T_CTX // tm
    per_seq = DEC_SEQ // tm
    cond = lambda i, e: (jnp.where(i < n_ctx, 0, 1 + (i - n_ctx) // per_seq), 0, 0)
    return pl.pallas_call(
        _moe_kernel,
        out_shape=jax.ShapeDtypeStruct((T_ALL, D_MODEL), F32),
        grid=(T_ALL // tm, N_EXPERTS),
        in_specs=[
            pl.BlockSpec((tm, D_MODEL), lambda i, e: (i, 0)),
            pl.BlockSpec((tm, N_EXPERTS), lambda i, e: (i, 0)),
            pl.BlockSpec((None, D_MODEL, D_EXPERT), lambda i, e: (e, 0, 0)),
            pl.BlockSpec((None, D_MODEL, D_EXPERT), lambda i, e: (e, 0, 0)),
            pl.BlockSpec((None, D_EXPERT, D_MODEL), lambda i, e: (e, 0, 0)),
            pl.BlockSpec((tm, D_MODEL), lambda i, e: (i, 0)),
            pl.BlockSpec((None, N_MOD, D_MODEL), cond),
        ],
        out_specs=pl.BlockSpec((tm, D_MODEL), lambda i, e: (i, 0)),
        scratch_shapes=[pltpu.VMEM((tm, D_MODEL), F32)],
        compiler_params=_params("parallel", "arbitrary"),
        name="experts",
    )(h, gates, wg, wu, wd, x, mod)


def _split_parity(a, batch, n):
    rest = a.shape[2:]
    return a.reshape(batch, n // 2, 2, *rest).swapaxes(1, 2).reshape(batch * n, *rest)


def _merge_parity(a, batch, n):
    rest = a.shape[1:]
    return a.reshape(batch, 2, n // 2, *rest).swapaxes(1, 2).reshape(batch, n, *rest)


def _dft_matrices(n):
    f = jnp.arange(n, dtype=jnp.int32)[:, None]
    t = jnp.arange(n, dtype=jnp.int32)[None, :]
    ang = ((f * t) % (2 * n)).astype(F32) * (math.pi / n)
    alt = jnp.where(t % 2 == 0, 1.0, -1.0).astype(F32)
    fwd = jnp.concatenate([jnp.cos(ang), alt, -jnp.sin(ang)[1:]], axis=0)
    row = jnp.arange(2 * n, dtype=jnp.int32)[:, None]
    weight = jnp.where((row == 0) | (row == n), 0.5 / n, 1.0 / n).astype(F32)
    inv = (fwd * weight).T
    return fwd, inv


def _conv_constants(n):
    half = n // 2
    fwd, inv = _dft_matrices(half)
    ang = jnp.arange(half, dtype=F32)[:, None] * (math.pi / n)
    cos = jnp.broadcast_to(jnp.cos(ang), (half, HY_CT))
    sin = jnp.broadcast_to(jnp.sin(ang), (half, HY_CT))
    return (fwd.astype(BF16), (0.5 * inv).astype(BF16)), (cos, sin)


def _filter_taps(n, w1, b1, w2, b2, w3, b3, freq):
    hp = lax.Precision.HIGHEST
    t01 = jnp.linspace(0.0, 1.0, n, dtype=F32)
    w = (2.0 * math.pi / n) * jnp.arange(n, dtype=F32)
    bands = jnp.linspace(1e-4, FILT_BANDS - 1, FILT_BANDS, dtype=F32)
    z = jnp.concatenate([t01[:, None], jnp.cos(w[:, None] * bands[None, :]),
                         -jnp.sin(w[:, None] * bands[None, :])], axis=-1)
    h = jnp.sin(freq * (jnp.dot(z, w1, precision=hp) + b1))
    h = jnp.sin(freq * (jnp.dot(h, w2, precision=hp) + b2))
    h = (jnp.dot(h, w3, precision=hp) + b3).reshape(n, HY_ORDER, 2, HY_DIM)
    deltas = jnp.abs(jnp.linspace(HY_MIN_DECAY, HY_MAX_DECAY, HY_DIM, dtype=F32))
    h = h * jnp.exp(-t01[:, None] * deltas[None, :])[:, None, None, :]
    h_fwd = h[:, :, 0]
    h_bwd = h[:, :, 1] * (jnp.arange(n) > 0)[:, None, None].astype(F32)
    scale = lax.rsqrt(jnp.sum(h_fwd * h_fwd, axis=0, keepdims=True) + jnp.sum(h_bwd * h_bwd, axis=0, keepdims=True) + EPS)
    return h_fwd * scale, h_bwd * scale


def _filter_tables(n, fwd, twiddle, filt_args):
    half = n // 2
    h_fwd, h_bwd = _filter_taps(n, *filt_args)
    cols = HY_ORDER * HY_DIM
    taps = jnp.concatenate([h_fwd.reshape(n, cols), h_bwd.reshape(n, cols)], axis=1)
    eo = _matmul(fwd, jnp.concatenate([taps[0::2], taps[1::2]], axis=1).astype(BF16))
    e, o = eo[:, :2 * cols], eo[:, 2 * cols:]
    er, ei, orr, oi = e[:half], e[half:], o[:half], o[half:]
    cos, sin = twiddle[0][:, :1], twiddle[1][:, :1]
    tr = cos * orr + sin * oi
    ti = cos * oi - sin * orr
    f_, b_ = slice(0, cols), slice(cols, 2 * cols)
    har = (er + tr)[:, f_] + (er + tr)[:, b_]
    hai = (ei + ti)[:, f_] - (ei + ti)[:, b_]
    hbr = (er - tr)[:, f_] + (er - tr)[:, b_]
    hbi = (ei - ti)[:, f_] - (ei - ti)[:, b_]
    first = (jnp.arange(half) == 0)[:, None]
    hai = jnp.where(first, ei[0:1, f_] + ei[0:1, b_], hai)
    hbi = jnp.where(first, oi[0:1, b_] - oi[0:1, f_], hbi)
    tabs = jnp.stack([har, hai, hbr, hbi], axis=0).reshape(4, half, HY_ORDER, HY_DIM)
    return tabs.transpose(2, 0, 1, 3)


def _positions(n):
    r = jnp.arange(n, dtype=jnp.int32)
    return jnp.where(r < n // 2, 2 * r, 2 * (r - n // 2) + 1)


def _rope_tables(first_lane, width, period):
    pos = _positions(DEC_SEQ)
    lane = jnp.arange(LANES, dtype=jnp.int32)
    d = lane % period - first_lane
    active = (d >= 0) & (d < width)
    axis_w = width // 2
    npair = axis_w // 2
    e = d % axis_w
    coord = jnp.where(d < axis_w, (pos // GRID_W)[:, None], (pos % GRID_W)[:, None]).astype(F32)
    freqs = ROPE_THETA ** (-(e % npair).astype(F32) / npair)
    ang = coord * freqs[None, :]
    cos = jnp.where(active[None, :], jnp.cos(ang), 1.0)
    sin = jnp.where(active[None, :], jnp.sin(ang), 0.0)
    second = (e >= npair)[None, :]
    sin_up = jnp.where(second, sin, 0.0)
    sin_dn = jnp.where(second, 0.0, -sin)
    ident = lambda v: jnp.full((TM, LANES), v, F32)
    return (jnp.concatenate([ident(1.0), cos], axis=0), jnp.concatenate([ident(0.0), sin_up], axis=0),
            jnp.concatenate([ident(0.0), sin_dn], axis=0))


def _fold_gain(tabs, gain, partner, scale=1.0):
    cos, sin_up, sin_dn = tabs
    g = gain.reshape(1, LANES) * scale
    return cos * g, sin_up * jnp.roll(g, partner, axis=1), sin_dn * jnp.roll(g, -partner, axis=1)


def _kv_head_of_slot():
    return jax.nn.one_hot(jnp.arange(GQA_HEADS) // GQA_GROUP, GQA_KV_HEADS, dtype=F32)


def _even_weights(w_in, w_out):
    w_hy, w_q, w_k, w_v = jnp.split(w_in, [HY_IN, HY_IN + GQA_Q, HY_IN + GQA_Q + GQA_KV], axis=1)
    sel = _kv_head_of_slot()
    w_q = (w_q.reshape(D_MODEL, GQA_HEADS, 1, GQA_HEAD_DIM) * sel[None, :, :, None]).reshape(D_MODEL, GQA_SLOTS)
    w_in_p = jnp.concatenate([w_hy, w_q, w_k, w_v], axis=1).astype(BF16)
    w_o_hy, w_o_att = w_out[:HY_DIM], w_out[HY_DIM:]
    w_o_att = (w_o_att.reshape(GQA_HEADS, 1, GQA_HEAD_DIM, D_MODEL) * sel[:, :, None, None]).reshape(GQA_SLOTS, D_MODEL)
    return w_in_p, w_o_hy.astype(BF16), w_o_att.astype(BF16)


def _odd_weights(w_in, w_uq, w_ukv, q_norm, k_norm):
    pad = LANES - MLA_QK_DIM
    w_cq, w_ckv, w_kr = jnp.split(w_in, [MLA_Q_RANK, MLA_Q_RANK + MLA_KV_RANK], axis=1)
    w_kr = jnp.pad(w_kr, ((0, 0), (MLA_NOPE_DIM, pad)))
    lane = jnp.arange(LANES, dtype=jnp.int32)
    d = lane - MLA_NOPE_DIM
    is_rope = (d >= 0) & (d < MLA_ROPE_DIM)
    npair = MLA_ROPE_DIM // 4
    partner = jnp.where(is_rope, jnp.where(d % (2 * npair) < npair, lane + npair, lane - npair), lane)
    partner_cols = lambda w: jnp.where(is_rope, jnp.take(w, partner, axis=-1), 0.0)
    w_in_p = jnp.concatenate([w_cq, w_ckv, w_kr, partner_cols(w_kr)], axis=1).astype(BF16)
    w_uq_p = jnp.pad(w_uq.reshape(MLA_Q_RANK, MLA_HEADS, MLA_QK_DIM), ((0, 0), (0, 0), (0, pad)))
    w_ukv = w_ukv.reshape(MLA_KV_RANK, MLA_HEADS, MLA_NOPE_DIM + MLA_V_DIM)
    w_uk_p = jnp.pad(w_ukv[:, :, :MLA_NOPE_DIM], ((0, 0), (0, 0), (0, LANES - MLA_NOPE_DIM)))
    w_uv = w_ukv[:, :, MLA_NOPE_DIM:]
    slot_gain = lambda g: jnp.pad(g, (0, pad)).reshape(1, LANES)
    flat = lambda w, rows: w.reshape(rows, MLA_SLOTS).astype(BF16)
    return (w_in_p, flat(w_uq_p, MLA_Q_RANK), flat(partner_cols(w_uq_p), MLA_Q_RANK), flat(w_uk_p, MLA_KV_RANK),
            w_uv.reshape(MLA_KV_RANK, MLA_HEADS * MLA_V_DIM).astype(BF16), slot_gain(q_norm), slot_gain(k_norm))


def kernel(x_prompt, x_sample, cache_gqa_k, cache_gqa_v, cache_mla_ckv, cache_mla_krope, c, c_ctx, norm_mix, norm_ffn, w_mod, b_mod, ev_w_in, ev_conv_w, ev_conv_b, ev_filt_w1, ev_filt_b1, ev_filt_w2, ev_filt_b2, ev_filt_w3, ev_filt_b3, ev_filt_freq, ev_hy_bias, ev_q_norm, ev_k_norm, ev_w_out, od_w_in, od_q_lora_norm, od_kv_lora_norm, od_w_uq, od_w_ukv, od_q_norm, od_k_norm, od_w_out, router_w, router_bias, moe_w_gate, moe_w_up, moe_w_down):
    x = jnp.concatenate([_split_parity(x_prompt, BATCH, SEQ), _split_parity(x_sample, DEC_BATCH, DEC_SEQ)], axis=0)
    cond = jnp.concatenate([c_ctx[None, :], c], axis=0)
    cond = jnp.pad(cond, ((0, COND_ROWS - N_COND), (0, 0)))
    mod_all = _modulation(cond, w_mod, b_mod).reshape(DEPTH, COND_ROWS, N_MOD, D_MODEL)

    perm = jnp.arange(N_EXPERTS).reshape(N_GROUPS, EXPERTS_PER_GROUP).T.reshape(-1)
    rw_pm = router_w.T[perm]
    rb_pm = router_bias[perm].reshape(N_EXPERTS, 1)

    dft_ctx, tw_ctx = _conv_constants(SEQ)
    dft_lat, tw_lat = _conv_constants(DEC_SEQ)
    rope_gqa = _rope_tables(0, GQA_HEAD_DIM, GQA_HEAD_DIM)
    rope_mla = _rope_tables(MLA_NOPE_DIM, MLA_ROPE_DIM, LANES)
    ctx_blocks_lat = T_CTX // DEC_SEQ

    st_k, st_v, st_ckv, st_kr = [], [], [], []
    for l in range(DEPTH):
        i = l // 2
        mod = mod_all[l]
        if l % 2 == 0:
            w_in, w_o_hy, w_o_att = _even_weights(ev_w_in[i], ev_w_out[i])
            both = lambda g: jnp.concatenate([g, g]).reshape(1, LANES)
            q_tabs = _fold_gain(rope_gqa, both(ev_q_norm[i]), GQA_HEAD_DIM // 4, GQA_HEAD_DIM ** -0.5 * LOG2E)
            k_tabs = _fold_gain(rope_gqa, both(ev_k_norm[i]), GQA_HEAD_DIM // 4)
            u_hy, q, k_ro, v_bf, k_n, v_32 = _even_in(x, norm_mix[l], mod, w_in, both(ev_k_norm[i]), q_tabs, k_tabs)
            filt_args = (ev_filt_w1[i], ev_filt_b1[i], ev_filt_w2[i], ev_filt_b2[i], ev_filt_w3[i], ev_filt_b3[i],
                         ev_filt_freq[i])
            y_ctx = _hyena(u_hy, 0, SEQ, BATCH, ev_conv_w[i], ev_conv_b[i], dft_ctx, tw_ctx,
                           _filter_tables(SEQ, dft_ctx[0], tw_ctx, filt_args), ev_hy_bias[i])
            y_lat = _hyena(u_hy, ctx_blocks_lat, DEC_SEQ, DEC_BATCH, ev_conv_w[i], ev_conv_b[i], dft_lat, tw_lat,
                           _filter_tables(DEC_SEQ, dft_lat[0], tw_lat, filt_args), ev_hy_bias[i])
            o_ctx = _gqa_attention(q, 0, SEQ, BATCH, SEQ, (k_ro, v_bf))
            cache = (cache_gqa_k[:, i].reshape(DEC_BATCH, PAST_LEN, LANES), cache_gqa_v[:, i].reshape(DEC_BATCH, PAST_LEN, LANES))
            o_lat = _gqa_attention(q, ctx_blocks_lat, DEC_SEQ, DEC_BATCH, 128, (k_ro, v_bf), cache)
            st_k.append(k_n[:T_CTX])
            st_v.append(v_32[:T_CTX])
            x = _out_proj([y_ctx, o_ctx], [y_lat, o_lat], [w_o_hy, w_o_att], x, mod)
        else:
            w_in, w_uq, w_uq_partner, w_uk, w_uv, qg, kg = _odd_weights(od_w_in[i], od_w_uq[i], od_w_ukv[i],
                                                                        od_q_norm[i], od_k_norm[i])
            cos_sin = lambda tabs: (tabs[0], tabs[1] + tabs[2])
            q_tabs = cos_sin(_fold_gain(rope_mla, qg, MLA_ROPE_DIM // 4, MLA_QK_DIM ** -0.5 * LOG2E))
            k_tabs = cos_sin(_fold_gain(rope_mla, kg, MLA_ROPE_DIM // 4))
            q, k, v, ckv, kpe = _odd_in(x, norm_mix[l], mod, w_in, od_q_lora_norm[i].reshape(1, -1),
                                        od_kv_lora_norm[i].reshape(1, -1), w_uq, w_uq_partner, w_uk, w_uv,
                                        q_tabs, k_tabs)
            st_ckv.append(ckv[:T_CTX])
            st_kr.append(kpe[:T_CTX, MLA_NOPE_DIM:MLA_QK_DIM])
            cache_kpe = jnp.pad(cache_mla_krope[:, i].reshape(DEC_BATCH * PAST_LEN, MLA_ROPE_DIM),
                                ((0, 0), (MLA_NOPE_DIM, LANES - MLA_QK_DIM)))
            cache = _mla_cache(cache_mla_ckv[:, i].reshape(DEC_BATCH * PAST_LEN, MLA_KV_RANK), cache_kpe, w_uk, w_uv, kg)
            o_ctx = _mla_attention(q, k, v, 0, SEQ, BATCH, SEQ)
            o_lat = _mla_attention(q, k, v, ctx_blocks_lat, DEC_SEQ, DEC_BATCH, 512, cache)
            x = _out_proj([o_ctx], [o_lat], [od_w_out[i].astype(BF16)], x, mod)

        h, gates_pm = _router(x, norm_ffn[l], mod, rw_pm, rb_pm)
        gates = gates_pm.reshape(EXPERTS_PER_GROUP, N_GROUPS, T_ALL).transpose(2, 1, 0).reshape(T_ALL, N_EXPERTS)
        x = _moe(h, gates, moe_w_gate[l], moe_w_up[l], moe_w_down[l], x, mod)

    y_prompt = _merge_parity(x[:T_CTX], BATCH, SEQ)
    y_sample = _merge_parity(x[T_CTX:], DEC_BATCH, DEC_SEQ)
    state = lambda parts, shape: jnp.stack([_merge_parity(p, BATCH, SEQ) for p in parts], axis=1).reshape(shape)
    return (y_prompt, y_sample,
            state(st_k, (BATCH, DEPTH // 2, SEQ, GQA_KV_HEADS, GQA_HEAD_DIM)),
            state(st_v, (BATCH, DEPTH // 2, SEQ, GQA_KV_HEADS, GQA_HEAD_DIM)),
            state(st_ckv, (BATCH, DEPTH // 2, SEQ, MLA_KV_RANK)),
            state(st_kr, (BATCH, DEPTH // 2, SEQ, MLA_ROPE_DIM)))
```

```python
import functools
import math

import jax
import jax.numpy as jnp
from jax import lax
from jax.experimental import pallas as pl
from jax.experimental.pallas import tpu as pltpu

F32 = jnp.float32
BF16 = jnp.bfloat16

D_MODEL = 1024
BATCH = 16
SEQ = 256
DEPTH = 4
DEC_BATCH = 8
DEC_SEQ = 2048
PAST_LEN = 512
GRID_W = 64
EPS = 1e-6
ROPE_THETA = 10000.0
N_MOD = 6
HY_DIM = D_MODEL // 2
HY_ORDER = 2
FILT_EMB = 33
FILT_BANDS = (FILT_EMB - 1) // 2
HY_MIN_DECAY = math.log(1e-2) / 1.5
HY_MAX_DECAY = math.log(1e-2) / 0.3
GQA_HEADS = 8
GQA_KV_HEADS = 2
GQA_HEAD_DIM = 64
GQA_GROUP = GQA_HEADS // GQA_KV_HEADS
GQA_Q = GQA_HEADS * GQA_HEAD_DIM
GQA_KV = GQA_KV_HEADS * GQA_HEAD_DIM
HY_IN = (HY_ORDER + 1) * HY_DIM
MLA_HEADS = 16
MLA_NOPE_DIM = 64
MLA_ROPE_DIM = 32
MLA_QK_DIM = MLA_NOPE_DIM + MLA_ROPE_DIM
MLA_V_DIM = 64
MLA_Q_RANK = 256
MLA_KV_RANK = 128
N_EXPERTS = 16
N_GROUPS = 4
EXPERTS_PER_GROUP = N_EXPERTS // N_GROUPS
D_EXPERT = 256

LANES = 128
T_CTX = BATCH * SEQ
T_LAT = DEC_BATCH * DEC_SEQ
T_ALL = T_CTX + T_LAT
N_COND = 1 + DEC_BATCH
COND_ROWS = 16
TM = 512
N_CTX_TILES = T_CTX // TM
TILES_PER_SEQ = DEC_SEQ // TM
GQA_SLOTS = GQA_HEADS * LANES
MLA_SLOTS = MLA_HEADS * LANES
EVEN_COLS = HY_IN + GQA_SLOTS + 2 * LANES
HY_CT = 128
KV_CHUNK = 2048
LOG2E = math.log2(math.e)

VMEM_LIMIT_BYTES = 48 * 1024 * 1024


def _params(*sem):
    return pltpu.CompilerParams(dimension_semantics=sem, vmem_limit_bytes=VMEM_LIMIT_BYTES)


def _cond_index(i):
    return (jnp.where(i < N_CTX_TILES, 0, 1 + (i - N_CTX_TILES) // TILES_PER_SEQ), 0, 0)


def _rope_index(i):
    return (jnp.where(i < N_CTX_TILES, 0, 1 + (i - N_CTX_TILES) % TILES_PER_SEQ), 0)


def _ctx_index(i):
    return (jnp.minimum(i, N_CTX_TILES - 1), 0)


def _lat_index(i):
    return (jnp.maximum(i - N_CTX_TILES, 0), 0)


def _const2(i):
    return (0, 0)


def _norm_mod(x, gain, mod_ref, shift_row, scale_row):
    r = lax.rsqrt(jnp.mean(x * x, axis=-1, keepdims=True) + EPS)
    h = (x * r) * gain
    return h * (1.0 + mod_ref[scale_row:scale_row + 1, :]) + mod_ref[shift_row:shift_row + 1, :]


def _slot_norm(t, dims, gain):
    r = lax.rsqrt(jnp.sum(t * t, axis=-1, keepdims=True) / dims + EPS)
    return (t * r) * gain


def _slot_rms(t, dims):
    return lax.rsqrt(jnp.sum(t * t, axis=-1, keepdims=True) / dims + EPS)


def _rope(t, tabs, partner):
    cos, sin_up, sin_dn = tabs
    return t * cos + pltpu.roll(t, partner, 1) * sin_up + pltpu.roll(t, LANES - partner, 1) * sin_dn


def _mod_kernel(c_ref, w_ref, b_ref, o_ref):
    c = c_ref[...]
    a = (c * jax.nn.sigmoid(c)).astype(BF16)
    o_ref[...] = jnp.dot(a, w_ref[...].astype(BF16), preferred_element_type=F32) + b_ref[...]


def _modulation(cond, w_mod, b_mod):
    tn = 1536
    return pl.pallas_call(
        _mod_kernel,
        out_shape=jax.ShapeDtypeStruct((DEPTH, COND_ROWS, N_MOD * D_MODEL), F32),
        grid=(DEPTH, (N_MOD * D_MODEL) // tn),
        in_specs=[
            pl.BlockSpec((COND_ROWS, D_MODEL), lambda l, j: (0, 0)),
            pl.BlockSpec((None, D_MODEL, tn), lambda l, j: (l, 0, j)),
            pl.BlockSpec((None, 1, tn), lambda l, j: (l, 0, j)),
        ],
        out_specs=pl.BlockSpec((None, COND_ROWS, tn), lambda l, j: (l, 0, j)),
        compiler_params=_params("parallel", "parallel"),
        name="modulation",
    )(cond, w_mod, b_mod.reshape(DEPTH, 1, N_MOD * D_MODEL))


def _even_in_kernel(x_ref, g_ref, mod_ref, w_ref, kg_ref, qc_ref, qu_ref, qd_ref, kc_ref, ku_ref, kd_ref,
                    uhy_ref, q_ref, kro_ref, vbf_ref, kn_ref, v32_ref):
    h = _norm_mod(x_ref[...], g_ref[...], mod_ref, 0, 1)
    u = jnp.dot(h.astype(BF16), w_ref[...], preferred_element_type=F32)
    uhy_ref[...] = u[:, :HY_IN].astype(BF16)
    qtabs = (qc_ref[...], qu_ref[...], qd_ref[...])
    partner = GQA_HEAD_DIM // 4
    for s in range(GQA_HEADS):
        qs = u[:, HY_IN + s * LANES:HY_IN + (s + 1) * LANES]
        q_ref[:, s * LANES:(s + 1) * LANES] = (_rope(qs, qtabs, partner) * _slot_rms(qs, GQA_HEAD_DIM)).astype(BF16)
    k = u[:, HY_IN + GQA_SLOTS:HY_IN + GQA_SLOTS + LANES]
    lo = lax.broadcasted_iota(jnp.int32, k.shape, 1) < GQA_HEAD_DIM
    k2 = k * k
    ss_lo = jnp.sum(jnp.where(lo, k2, 0.0), axis=-1, keepdims=True)
    ss_hi = jnp.sum(jnp.where(lo, 0.0, k2), axis=-1, keepdims=True)
    r = jnp.where(lo, lax.rsqrt(ss_lo / GQA_HEAD_DIM + EPS), lax.rsqrt(ss_hi / GQA_HEAD_DIM + EPS))
    kn_ref[...] = (k * r) * kg_ref[...]
    kro_ref[...] = (_rope(k, (kc_ref[...], ku_ref[...], kd_ref[...]), partner) * r).astype(BF16)
    v = u[:, HY_IN + GQA_SLOTS + LANES:]
    v32_ref[...] = v
    vbf_ref[...] = v.astype(BF16)


def _even_in(x, gain, mod, w, kg, qtabs, ktabs):
    tok = lambda n, dt: jax.ShapeDtypeStruct((T_ALL, n), dt)
    row = lambda n: pl.BlockSpec((TM, n), lambda i: (i, 0))
    tab = pl.BlockSpec((TM, LANES), _rope_index)
    return pl.pallas_call(
        _even_in_kernel,
        out_shape=(tok(HY_IN, BF16), tok(GQA_SLOTS, BF16), tok(LANES, BF16), tok(LANES, BF16),
                   tok(LANES, F32), tok(LANES, F32)),
        grid=(T_ALL // TM,),
        in_specs=[row(D_MODEL), pl.BlockSpec((1, D_MODEL), _const2), pl.BlockSpec((None, N_MOD, D_MODEL), _cond_index),
                  pl.BlockSpec((D_MODEL, EVEN_COLS), _const2), pl.BlockSpec((1, LANES), _const2),
                  tab, tab, tab, tab, tab, tab],
        out_specs=(row(HY_IN), row(GQA_SLOTS), row(LANES), row(LANES), row(LANES), row(LANES)),
        compiler_params=_params("parallel"),
        name="even_in",
    )(x, gain.reshape(1, D_MODEL), mod, w, kg, *qtabs, *ktabs)


def _odd_in_kernel(x_ref, g_ref, mod_ref, win_ref, qln_ref, kvln_ref, wuq_ref, wuqp_ref, wuk_ref, wuv_ref,
                   qc_ref, qs_ref, kc_ref, ks_ref, q_ref, k_ref, v_ref, ckv_ref, kpe_ref):
    h = _norm_mod(x_ref[...], g_ref[...], mod_ref, 0, 1)
    u = jnp.dot(h.astype(BF16), win_ref[...], preferred_element_type=F32)
    cq = u[:, :MLA_Q_RANK]
    cq = (cq * lax.rsqrt(jnp.mean(cq * cq, axis=-1, keepdims=True) + EPS)) * qln_ref[...]
    ckv = u[:, MLA_Q_RANK:MLA_Q_RANK + MLA_KV_RANK]
    ckv = (ckv * lax.rsqrt(jnp.mean(ckv * ckv, axis=-1, keepdims=True) + EPS)) * kvln_ref[...]
    kpe = u[:, MLA_Q_RANK + MLA_KV_RANK:MLA_Q_RANK + MLA_KV_RANK + LANES]
    kpe_partner = u[:, MLA_Q_RANK + MLA_KV_RANK + LANES:]
    ckv_ref[...] = ckv
    kpe_ref[...] = kpe
    ckv_b = ckv.astype(BF16)
    cq_b = cq.astype(BF16)
    v_ref[...] = jnp.dot(ckv_b, wuv_ref[...], preferred_element_type=F32).astype(BF16)
    q_all = jnp.dot(cq_b, wuq_ref[...], preferred_element_type=F32)
    q_partner = jnp.dot(cq_b, wuqp_ref[...], preferred_element_type=F32)
    k_all = jnp.dot(ckv_b, wuk_ref[...], preferred_element_type=F32)
    qcos, qsin, kcos, ksin = qc_ref[...], qs_ref[...], kc_ref[...], ks_ref[...]
    k_rot = kpe_partner * ksin
    for s in range(MLA_HEADS):
        sl = slice(s * LANES, (s + 1) * LANES)
        qs = q_all[:, sl]
        q_ref[:, sl] = ((qs * qcos + q_partner[:, sl] * qsin) * _slot_rms(qs, MLA_QK_DIM)).astype(BF16)
        ks = k_all[:, sl] + kpe
        k_ref[:, sl] = ((ks * kcos + k_rot) * _slot_rms(ks, MLA_QK_DIM)).astype(BF16)


def _odd_in(x, gain, mod, win, qln, kvln, wuq, wuqp, wuk, wuv, qtabs, ktabs):
    tok = lambda n, dt: jax.ShapeDtypeStruct((T_ALL, n), dt)
    row = lambda n: pl.BlockSpec((TM, n), lambda i: (i, 0))
    full = lambda a: pl.BlockSpec(a.shape, _const2)
    tab = pl.BlockSpec((TM, LANES), _rope_index)
    return pl.pallas_call(
        _odd_in_kernel,
        out_shape=(tok(MLA_SLOTS, BF16), tok(MLA_SLOTS, BF16), tok(MLA_HEADS * MLA_V_DIM, BF16),
                   tok(LANES, F32), tok(LANES, F32)),
        grid=(T_ALL // TM,),
        in_specs=[row(D_MODEL), pl.BlockSpec((1, D_MODEL), _const2), pl.BlockSpec((None, N_MOD, D_MODEL), _cond_index),
                  full(win), full(qln), full(kvln), full(wuq), full(wuqp), full(wuk), full(wuv), tab, tab, tab, tab],
        out_specs=(row(MLA_SLOTS), row(MLA_SLOTS), row(MLA_HEADS * MLA_V_DIM), row(LANES), row(LANES)),
        compiler_params=_params("parallel"),
        name="odd_in",
    )(x, gain.reshape(1, D_MODEL), mod, win, qln, kvln, wuq, wuqp, wuk, wuv, *qtabs, *ktabs)


def _mla_cache_kernel(ckv_ref, kpe_ref, wuk_ref, wuv_ref, kg_ref, k_ref, v_ref):
    ckv_b = ckv_ref[...].astype(BF16)
    v_ref[...] = jnp.dot(ckv_b, wuv_ref[...], preferred_element_type=F32).astype(BF16)
    k_all = jnp.dot(ckv_b, wuk_ref[...], preferred_element_type=F32)
    kpe, kg = kpe_ref[...], kg_ref[...]
    for s in range(MLA_HEADS):
        sl = slice(s * LANES, (s + 1) * LANES)
        k_ref[:, sl] = _slot_norm(k_all[:, sl] + kpe, MLA_QK_DIM, kg).astype(BF16)


def _mla_cache(ckv, kpe, wuk, wuv, kg):
    rows = ckv.shape[0]
    row = lambda n: pl.BlockSpec((TM, n), lambda i: (i, 0))
    full = lambda a: pl.BlockSpec(a.shape, _const2)
    return pl.pallas_call(
        _mla_cache_kernel,
        out_shape=(jax.ShapeDtypeStruct((rows, MLA_SLOTS), BF16), jax.ShapeDtypeStruct((rows, MLA_HEADS * MLA_V_DIM), BF16)),
        grid=(rows // TM,),
        in_specs=[row(LANES), row(LANES), full(wuk), full(wuv), full(kg)],
        out_specs=(row(MLA_SLOTS), row(MLA_HEADS * MLA_V_DIM)),
        compiler_params=_params("parallel"),
        name="mla_cache",
    )(ckv, kpe, wuk, wuv, kg)


def _out_kernel(*refs, n_parts):
    ctx = refs[:n_parts]
    lat = refs[n_parts:2 * n_parts]
    ws = refs[2 * n_parts:3 * n_parts]
    x_ref, mod_ref, o_ref = refs[3 * n_parts:]
    i = pl.program_id(0)

    def emit(parts):
        y = jnp.dot(parts[0][...], ws[0][...], preferred_element_type=F32)
        for p, w in zip(parts[1:], ws[1:]):
            y += jnp.dot(p[...], w[...], preferred_element_type=F32)
        o_ref[...] = x_ref[...] + mod_ref[2:3, :] * y

    @pl.when(i < N_CTX_TILES)
    def _():
        emit(ctx)

    @pl.when(i >= N_CTX_TILES)
    def _():
        emit(lat)


def _out_proj(parts_ctx, parts_lat, ws, x, mod):
    n_parts = len(ws)
    specs = [pl.BlockSpec((TM, p.shape[1]), _ctx_index) for p in parts_ctx]
    specs += [pl.BlockSpec((TM, p.shape[1]), _lat_index) for p in parts_lat]
    specs += [pl.BlockSpec(w.shape, _const2) for w in ws]
    specs += [pl.BlockSpec((TM, D_MODEL), lambda i: (i, 0)), pl.BlockSpec((None, N_MOD, D_MODEL), _cond_index)]
    return pl.pallas_call(
        functools.partial(_out_kernel, n_parts=n_parts),
        out_shape=jax.ShapeDtypeStruct((T_ALL, D_MODEL), F32),
        grid=(T_ALL // TM,),
        in_specs=specs,
        out_specs=pl.BlockSpec((TM, D_MODEL), lambda i: (i, 0)),
        compiler_params=_params("parallel"),
        name="out_proj",
    )(*parts_ctx, *parts_lat, *ws, x, mod)


def _mm_kernel(a_ref, x_ref, o_ref):
    o_ref[...] = jnp.dot(a_ref[...], x_ref[...], preferred_element_type=F32)


def _matmul(a, x, tn=1024):
    m, k = a.shape
    n = x.shape[1]
    return pl.pallas_call(
        _mm_kernel,
        out_shape=jax.ShapeDtypeStruct((m, n), F32),
        grid=(n // tn,),
        in_specs=[pl.BlockSpec((m, k), _const2), pl.BlockSpec((k, tn), lambda j: (0, j))],
        out_specs=pl.BlockSpec((m, tn), lambda j: (0, j)),
        compiler_params=_params("parallel"),
        name="taps_dft",
    )(a, x)


def _hyena_kernel(v_ref, x1_ref, x2_ref, cw_ref, cb_ref, f_ref, g_ref, tc_ref, ts_ref, h_ref, bias_ref, o_ref):
    n, ct = v_ref.shape
    half = n // 2
    row0 = lax.broadcasted_iota(jnp.int32, (half, ct), 0) == 0
    row_last = lax.broadcasted_iota(jnp.int32, (half, ct), 0) == half - 1
    cos, sin = tc_ref[...], ts_ref[...]

    def conv3(ref, grp):
        s = ref[...].astype(F32)
        se, so = s[:half], s[half:]
        w0, w1, w2 = cw_ref[0, grp:grp + 1, :], cw_ref[1, grp:grp + 1, :], cw_ref[2, grp:grp + 1, :]
        b = cb_ref[grp:grp + 1, :]
        so_prev = jnp.where(row0, 0.0, pltpu.roll(so, 1, 0))
        se_next = jnp.where(row_last, 0.0, pltpu.roll(se, half - 1, 0))
        return (w0 * so_prev + w1 * se + w2 * so + b, w0 * se + w1 * so + w2 * se_next + b)

    def long_conv(ze, zo, order):
        eo = jnp.dot(f_ref[...], jnp.concatenate([ze, zo], axis=1).astype(BF16), preferred_element_type=F32)
        er, ei, orr, oi = eo[:half, :ct], eo[half:, :ct], eo[:half, ct:], eo[half:, ct:]
        har, hai, hbr, hbi = h_ref[order, 0], h_ref[order, 1], h_ref[order, 2], h_ref[order, 3]
        tr = cos * orr + sin * oi
        ti = cos * oi - sin * orr
        pr, pi_, mr, mi = er + tr, ei + ti, er - tr, ei - ti
        ypr = pr * har - pi_ * hai
        ypi = pr * hai + pi_ * har
        ymr = mr * hbr - mi * hbi
        ymi = mr * hbi + mi * hbr
        ar, ai = ypr + ymr, ypi + ymi
        dr, di = ypr - ymr, ypi - ymi
        br = cos * dr - sin * di
        bi = cos * di + sin * dr
        e0, o0, en, on = er[0:1], orr[0:1], ei[0:1], oi[0:1]
        p0 = (e0 + o0) * har[0:1]
        m0 = (e0 - o0) * hbr[0:1]
        ar = jnp.where(row0, p0 + m0, ar)
        br = jnp.where(row0, p0 - m0, br)
        ai = jnp.where(row0, 2.0 * (en * hai[0:1] + on * hbi[0:1]), ai)
        bi = jnp.where(row0, -2.0 * (en * hbi[0:1] - on * hai[0:1]), bi)
        ab = jnp.concatenate([jnp.concatenate([ar, ai], axis=0), jnp.concatenate([br, bi], axis=0)], axis=1)
        y = jnp.dot(g_ref[...], ab.astype(BF16), preferred_element_type=F32)
        bias = bias_ref[order:order + 1, :]
        return y[:, :ct] + bias * ze, y[:, ct:] + bias * zo

    ve, vo = conv3(v_ref, 0)
    x1e, x1o = conv3(x1_ref, 1)
    x2e, x2o = conv3(x2_ref, 2)
    c1e, c1o = long_conv(ve, vo, 0)
    z1e, z1o = x1e * c1e, x1o * c1o
    c2e, c2o = long_conv(z1e, z1o, 1)
    o_ref[:half, :] = (x2e * c2e).astype(o_ref.dtype)
    o_ref[half:, :] = (x2o * c2o).astype(o_ref.dtype)


def _hyena(u_hy, row_block0, n, batch, conv_w, conv_b, dft, twiddle, tables, bias):
    fwd, inv = dft
    half = n // 2
    nj = HY_DIM // HY_CT
    seq = lambda grp: pl.BlockSpec((n, HY_CT), lambda b, j: (row_block0 + b, grp * nj + j))
    return pl.pallas_call(
        _hyena_kernel,
        out_shape=jax.ShapeDtypeStruct((batch * n, HY_DIM), BF16),
        grid=(batch, nj),
        in_specs=[seq(0), seq(1), seq(2),
                  pl.BlockSpec((3, HY_ORDER + 1, HY_CT), lambda b, j: (0, 0, j)),
                  pl.BlockSpec((HY_ORDER + 1, HY_CT), lambda b, j: (0, j)),
                  pl.BlockSpec((n, half), lambda b, j: (0, 0)),
                  pl.BlockSpec((half, n), lambda b, j: (0, 0)),
                  pl.BlockSpec((half, HY_CT), lambda b, j: (0, 0)),
                  pl.BlockSpec((half, HY_CT), lambda b, j: (0, 0)),
                  pl.BlockSpec((HY_ORDER, 4, half, HY_CT), lambda b, j: (0, 0, 0, j)),
                  pl.BlockSpec((HY_ORDER, HY_CT), lambda b, j: (0, j))],
        out_specs=pl.BlockSpec((n, HY_CT), lambda b, j: (b, j)),
        compiler_params=_params("parallel", "parallel"),
        name="hyena",
    )(u_hy, u_hy, u_hy, conv_w.reshape(3, HY_ORDER + 1, HY_DIM), conv_b.reshape(HY_ORDER + 1, HY_DIM),
      fwd, inv, twiddle[0], twiddle[1], tables, bias)


def _attend(q, chunks, vhalf):
    is_value = (lax.broadcasted_iota(jnp.int32, (1, LANES), 1) < LANES // 2) == (vhalf == 0)
    m = acc = None
    for load_k, load_v in chunks:
        s = lax.dot_general(q, load_k(), (((1,), (1,)), ((), ())), preferred_element_type=F32)
        v1 = jnp.where(is_value, load_v(), jnp.ones((), BF16))
        mc = jnp.max(s, axis=-1, keepdims=True)
        if m is None:
            m_new = mc
            acc = jnp.dot(jnp.exp2(s - m_new).astype(BF16), v1, preferred_element_type=F32)
        else:
            m_new = jnp.maximum(m, mc)
            acc = jnp.exp2(m - m_new) * acc + jnp.dot(jnp.exp2(s - m_new).astype(BF16), v1, preferred_element_type=F32)
        m = m_new
    denom = pltpu.roll(acc, LANES // 2, 1)
    return jnp.where(is_value, acc / denom, 0.0)


def _chunks(refs_kv, col):
    out = []
    for k_ref, v_ref in refs_kv:
        for c in range(k_ref.shape[0] // min(KV_CHUNK, k_ref.shape[0])):
            tk = min(KV_CHUNK, k_ref.shape[0])
            out.append((lambda k_ref=k_ref, c=c, tk=tk: k_ref[c * tk:(c + 1) * tk, col * LANES:(col + 1) * LANES].astype(BF16),
                        lambda v_ref=v_ref, c=c, tk=tk: v_ref[c * tk:(c + 1) * tk, :].astype(BF16)))
    return out


def _gqa_kernel(*refs):
    q_ref, o_ref = refs[0], refs[-1]
    kv = [(refs[i], refs[i + 1]) for i in range(1, len(refs) - 1, 2)]
    tq = q_ref.shape[0]
    for g in range(GQA_KV_HEADS):
        q = jnp.concatenate([q_ref[:, (g * GQA_GROUP + j) * LANES:(g * GQA_GROUP + j + 1) * LANES]
                             for j in range(GQA_GROUP)], axis=0)
        o = _attend(q, _chunks(kv, 0), g)
        for j in range(GQA_GROUP):
            s = g * GQA_GROUP + j
            o_ref[:, s * LANES:(s + 1) * LANES] = o[j * tq:(j + 1) * tq].astype(o_ref.dtype)


def _gqa_attention(q, row_block0, n, batch, tq, kv_new, kv_cache=None):
    nq = n // tq
    specs = [pl.BlockSpec((tq, GQA_SLOTS), lambda b, i: (row_block0 * nq + b * nq + i, 0))]
    args = [q]
    if kv_cache is not None:
        specs += [pl.BlockSpec((None, PAST_LEN, LANES), lambda b, i: (b, 0, 0))] * 2
        args += list(kv_cache)
    specs += [pl.BlockSpec((n, LANES), lambda b, i: (row_block0 + b, 0))] * 2
    args += list(kv_new)
    return pl.pallas_call(
        _gqa_kernel,
        out_shape=jax.ShapeDtypeStruct((batch * n, GQA_SLOTS), BF16),
        grid=(batch, nq),
        in_specs=specs,
        out_specs=pl.BlockSpec((tq, GQA_SLOTS), lambda b, i: (b * nq + i, 0)),
        compiler_params=_params("parallel", "parallel"),
        name="gqa_attention",
    )(*args)


def _mla_kernel(*refs):
    q_ref, o_ref = refs[0], refs[-1]
    kv = [(refs[i], refs[i + 1]) for i in range(1, len(refs) - 1, 2)]
    o_ref[...] = (_attend(q_ref[:, :LANES], _chunks(kv, 0), 0)
                  + _attend(q_ref[:, LANES:], _chunks(kv, 1), 1)).astype(o_ref.dtype)


def _mla_attention(q, k_new, v_new, row_block0, n, batch, tq, cache=None):
    nq = n // tq
    pairs = MLA_HEADS // 2
    specs = [pl.BlockSpec((tq, 2 * LANES), lambda b, j, i: (row_block0 * nq + b * nq + i, j))]
    args = [q]
    if cache is not None:
        specs += [pl.BlockSpec((PAST_LEN, 2 * LANES), lambda b, j, i: (b, j)),
                  pl.BlockSpec((PAST_LEN, LANES), lambda b, j, i: (b, j))]
        args += list(cache)
    specs += [pl.BlockSpec((n, 2 * LANES), lambda b, j, i: (row_block0 + b, j)),
              pl.BlockSpec((n, LANES), lambda b, j, i: (row_block0 + b, j))]
    args += [k_new, v_new]
    return pl.pallas_call(
        _mla_kernel,
        out_shape=jax.ShapeDtypeStruct((batch * n, MLA_HEADS * MLA_V_DIM), BF16),
        grid=(batch, pairs, nq),
        in_specs=specs,
        out_specs=pl.BlockSpec((tq, LANES), lambda b, j, i: (b * nq + i, j)),
        compiler_params=_params("parallel", "parallel", "parallel"),
        name="mla_attention",
    )(*args)


def _router_kernel(x_ref, g_ref, mod_ref, rw_ref, rb_ref, h_ref, gate_ref):
    h = _norm_mod(x_ref[...], g_ref[...], mod_ref, 3, 4)
    h_ref[...] = h.astype(BF16)
    logits = lax.dot_general(rw_ref[...], h, (((1,), (1,)), ((), ())),
                             precision=lax.Precision.HIGHEST, preferred_element_type=F32)
    score = jax.nn.sigmoid(logits)
    sel = score + rb_ref[...]
    ng = N_GROUPS
    sv = [sel[p * ng:(p + 1) * ng, :] for p in range(EXPERTS_PER_GROUP)]
    hi01, lo01 = jnp.maximum(sv[0], sv[1]), jnp.minimum(sv[0], sv[1])
    hi23, lo23 = jnp.maximum(sv[2], sv[3]), jnp.minimum(sv[2], sv[3])
    top1 = jnp.maximum(hi01, hi23)
    top2 = jnp.maximum(jnp.minimum(hi01, hi23), jnp.maximum(lo01, lo23))
    gscore = top1 + top2
    gidx = lax.broadcasted_iota(jnp.int32, gscore.shape, 0)
    beaten = jnp.zeros(gscore.shape, jnp.int32)
    for j in range(ng):
        other = gscore[j:j + 1, :]
        beaten += ((other > gscore) | ((other == gscore) & (j < gidx))).astype(jnp.int32)
    best = beaten == 0
    chosen = []
    for p in range(EXPERTS_PER_GROUP):
        rank = jnp.zeros(gscore.shape, jnp.int32)
        for pp in range(EXPERTS_PER_GROUP):
            if pp != p:
                wins = (sv[pp] > sv[p]) | ((sv[pp] == sv[p]) & (pp < p))
                rank += wins.astype(jnp.int32)
        chosen.append((rank < 2) & best)
    picked = [jnp.where(chosen[p], score[p * ng:(p + 1) * ng, :], 0.0) for p in range(EXPERTS_PER_GROUP)]
    total = jnp.sum(picked[0] + picked[1] + picked[2] + picked[3], axis=0, keepdims=True)
    gate_ref[...] = jnp.concatenate(picked, axis=0) / total


def _router(x, gain, mod, rw_pm, rb_pm):
    return pl.pallas_call(
        _router_kernel,
        out_shape=(jax.ShapeDtypeStruct((T_ALL, D_MODEL), BF16), jax.ShapeDtypeStruct((N_EXPERTS, T_ALL), F32)),
        grid=(T_ALL // TM,),
        in_specs=[
            pl.BlockSpec((TM, D_MODEL), lambda i: (i, 0)),
            pl.BlockSpec((1, D_MODEL), _const2),
            pl.BlockSpec((None, N_MOD, D_MODEL), _cond_index),
            pl.BlockSpec((N_EXPERTS, D_MODEL), _const2),
            pl.BlockSpec((N_EXPERTS, 1), _const2),
        ],
        out_specs=(pl.BlockSpec((TM, D_MODEL), lambda i: (i, 0)), pl.BlockSpec((N_EXPERTS, TM), lambda i: (0, i))),
        compiler_params=_params("parallel"),
        name="router",
    )(x, gain.reshape(1, D_MODEL), mod, rw_pm, rb_pm)


MOE_TM = 1024


def _moe_kernel(h_ref, gate_ref, wg_ref, wu_ref, wd_ref, x_ref, mod_ref, o_ref, acc_ref):
    e = pl.program_id(1)

    @pl.when(e == 0)
    def _():
        acc_ref[...] = jnp.zeros_like(acc_ref)

    h = h_ref[...]
    a = jnp.dot(h, wg_ref[...].astype(BF16), preferred_element_type=F32)
    u = jnp.dot(h, wu_ref[...].astype(BF16), preferred_element_type=F32)
    gates = gate_ref[...]
    lane = lax.broadcasted_iota(jnp.int32, gates.shape, 1)
    g = jnp.sum(jnp.where(lane == e, gates, 0.0), axis=1, keepdims=True)
    mid = (a * jax.nn.sigmoid(a)) * u * g
    acc_ref[...] += jnp.dot(mid.astype(BF16), wd_ref[...].astype(BF16), preferred_element_type=F32)

    @pl.when(e == N_EXPERTS - 1)
    def _():
        o_ref[...] = x_ref[...] + mod_ref[5:6, :] * acc_ref[...]


def _moe(h, gates, wg, wu, wd, x, mod):
    tm = MOE_TM
    n_ctx = T_CTX // tm
    per_seq = DEC_SEQ // tm
    cond = lambda i, e: (jnp.where(i < n_ctx, 0, 1 + (i - n_ctx) // per_seq), 0, 0)
    return pl.pallas_call(
        _moe_kernel,
        out_shape=jax.ShapeDtypeStruct((T_ALL, D_MODEL), F32),
        grid=(T_ALL // tm, N_EXPERTS),
        in_specs=[
            pl.BlockSpec((tm, D_MODEL), lambda i, e: (i, 0)),
            pl.BlockSpec((tm, N_EXPERTS), lambda i, e: (i, 0)),
            pl.BlockSpec((None, D_MODEL, D_EXPERT), lambda i, e: (e, 0, 0)),
            pl.BlockSpec((None, D_MODEL, D_EXPERT), lambda i, e: (e, 0, 0)),
            pl.BlockSpec((None, D_EXPERT, D_MODEL), lambda i, e: (e, 0, 0)),
            pl.BlockSpec((tm, D_MODEL), lambda i, e: (i, 0)),
            pl.BlockSpec((None, N_MOD, D_MODEL), cond),
        ],
        out_specs=pl.BlockSpec((tm, D_MODEL), lambda i, e: (i, 0)),
        scratch_shapes=[pltpu.VMEM((tm, D_MODEL), F32)],
        compiler_params=_params("parallel", "arbitrary"),
        name="experts",
    )(h, gates, wg, wu, wd, x, mod)


def _split_parity(a, batch, n):
    rest = a.shape[2:]
    return a.reshape(batch, n // 2, 2, *rest).swapaxes(1, 2).reshape(batch * n, *rest)


def _merge_parity(a, batch, n):
    rest = a.shape[1:]
    return a.reshape(batch, 2, n // 2, *rest).swapaxes(1, 2).reshape(batch, n, *rest)


def _dft_matrices(n):
    f = jnp.arange(n, dtype=jnp.int32)[:, None]
    t = jnp.arange(n, dtype=jnp.int32)[None, :]
    ang = ((f * t) % (2 * n)).astype(F32) * (math.pi / n)
    alt = jnp.where(t % 2 == 0, 1.0, -1.0).astype(F32)
    fwd = jnp.concatenate([jnp.cos(ang), alt, -jnp.sin(ang)[1:]], axis=0)
    row = jnp.arange(2 * n, dtype=jnp.int32)[:, None]
    weight = jnp.where((row == 0) | (row == n), 0.5 / n, 1.0 / n).astype(F32)
    inv = (fwd * weight).T
    return fwd, inv


def _conv_constants(n):
    half = n // 2
    fwd, inv = _dft_matrices(half)
    ang = jnp.arange(half, dtype=F32)[:, None] * (math.pi / n)
    cos = jnp.broadcast_to(jnp.cos(ang), (half, HY_CT))
    sin = jnp.broadcast_to(jnp.sin(ang), (half, HY_CT))
    return (fwd.astype(BF16), (0.5 * inv).astype(BF16)), (cos, sin)


def _filter_taps(n, w1, b1, w2, b2, w3, b3, freq):
    hp = lax.Precision.HIGHEST
    t01 = jnp.linspace(0.0, 1.0, n, dtype=F32)
    w = (2.0 * math.pi / n) * jnp.arange(n, dtype=F32)
    bands = jnp.linspace(1e-4, FILT_BANDS - 1, FILT_BANDS, dtype=F32)
    z = jnp.concatenate([t01[:, None], jnp.cos(w[:, None] * bands[None, :]),
                         -jnp.sin(w[:, None] * bands[None, :])], axis=-1)
    h = jnp.sin(freq * (jnp.dot(z, w1, precision=hp) + b1))
    h = jnp.sin(freq * (jnp.dot(h, w2, precision=hp) + b2))
    h = (jnp.dot(h, w3, precision=hp) + b3).reshape(n, HY_ORDER, 2, HY_DIM)
    deltas = jnp.abs(jnp.linspace(HY_MIN_DECAY, HY_MAX_DECAY, HY_DIM, dtype=F32))
    h = h * jnp.exp(-t01[:, None] * deltas[None, :])[:, None, None, :]
    h_fwd = h[:, :, 0]
    h_bwd = h[:, :, 1] * (jnp.arange(n) > 0)[:, None, None].astype(F32)
    scale = lax.rsqrt(jnp.sum(h_fwd * h_fwd, axis=0, keepdims=True) + jnp.sum(h_bwd * h_bwd, axis=0, keepdims=True) + EPS)
    return h_fwd * scale, h_bwd * scale


def _filter_tables(n, fwd, twiddle, filt_args):
    half = n // 2
    h_fwd, h_bwd = _filter_taps(n, *filt_args)
    cols = HY_ORDER * HY_DIM
    taps = jnp.concatenate([h_fwd.reshape(n, cols), h_bwd.reshape(n, cols)], axis=1)
    eo = _matmul(fwd, jnp.concatenate([taps[0::2], taps[1::2]], axis=1).astype(BF16))
    e, o = eo[:, :2 * cols], eo[:, 2 * cols:]
    er, ei, orr, oi = e[:half], e[half:], o[:half], o[half:]
    cos, sin = twiddle[0][:, :1], twiddle[1][:, :1]
    tr = cos * orr + sin * oi
    ti = cos * oi - sin * orr
    f_, b_ = slice(0, cols), slice(cols, 2 * cols)
    har = (er + tr)[:, f_] + (er + tr)[:, b_]
    hai = (ei + ti)[:, f_] - (ei + ti)[:, b_]
    hbr = (er - tr)[:, f_] + (er - tr)[:, b_]
    hbi = (ei - ti)[:, f_] - (ei - ti)[:, b_]
    first = (jnp.arange(half) == 0)[:, None]
    hai = jnp.where(first, ei[0:1, f_] + ei[0:1, b_], hai)
    hbi = jnp.where(first, oi[0:1, b_] - oi[0:1, f_], hbi)
    tabs = jnp.stack([har, hai, hbr, hbi], axis=0).reshape(4, half, HY_ORDER, HY_DIM)
    return tabs.transpose(2, 0, 1, 3)


def _positions(n):
    r = jnp.arange(n, dtype=jnp.int32)
    return jnp.where(r < n // 2, 2 * r, 2 * (r - n // 2) + 1)


def _rope_tables(first_lane, width, period):
    pos = _positions(DEC_SEQ)
    lane = jnp.arange(LANES, dtype=jnp.int32)
    d = lane % period - first_lane
    active = (d >= 0) & (d < width)
    axis_w = width // 2
    npair = axis_w // 2
    e = d % axis_w
    coord = jnp.where(d < axis_w, (pos // GRID_W)[:, None], (pos % GRID_W)[:, None]).astype(F32)
    freqs = ROPE_THETA ** (-(e % npair).astype(F32) / npair)
    ang = coord * freqs[None, :]
    cos = jnp.where(active[None, :], jnp.cos(ang), 1.0)
    sin = jnp.where(active[None, :], jnp.sin(ang), 0.0)
    second = (e >= npair)[None, :]
    sin_up = jnp.where(second, sin, 0.0)
    sin_dn = jnp.where(second, 0.0, -sin)
    ident = lambda v: jnp.full((TM, LANES), v, F32)
    return (jnp.concatenate([ident(1.0), cos], axis=0), jnp.concatenate([ident(0.0), sin_up], axis=0),
            jnp.concatenate([ident(0.0), sin_dn], axis=0))


def _fold_gain(tabs, gain, partner, scale=1.0):
    cos, sin_up, sin_dn = tabs
    g = gain.reshape(1, LANES) * scale
    return cos * g, sin_up * jnp.roll(g, partner, axis=1), sin_dn * jnp.roll(g, -partner, axis=1)


def _kv_head_of_slot():
    return jax.nn.one_hot(jnp.arange(GQA_HEADS) // GQA_GROUP, GQA_KV_HEADS, dtype=F32)


def _even_weights(w_in, w_out):
    w_hy, w_q, w_k, w_v = jnp.split(w_in, [HY_IN, HY_IN + GQA_Q, HY_IN + GQA_Q + GQA_KV], axis=1)
    sel = _kv_head_of_slot()
    w_q = (w_q.reshape(D_MODEL, GQA_HEADS, 1, GQA_HEAD_DIM) * sel[None, :, :, None]).reshape(D_MODEL, GQA_SLOTS)
    w_in_p = jnp.concatenate([w_hy, w_q, w_k, w_v], axis=1).astype(BF16)
    w_o_hy, w_o_att = w_out[:HY_DIM], w_out[HY_DIM:]
    w_o_att = (w_o_att.reshape(GQA_HEADS, 1, GQA_HEAD_DIM, D_MODEL) * sel[:, :, None, None]).reshape(GQA_SLOTS, D_MODEL)
    return w_in_p, w_o_hy.astype(BF16), w_o_att.astype(BF16)


def _odd_weights(w_in, w_uq, w_ukv, q_norm, k_norm):
    pad = LANES - MLA_QK_DIM
    w_cq, w_ckv, w_kr = jnp.split(w_in, [MLA_Q_RANK, MLA_Q_RANK + MLA_KV_RANK], axis=1)
    w_kr = jnp.pad(w_kr, ((0, 0), (MLA_NOPE_DIM, pad)))
    lane = jnp.arange(LANES, dtype=jnp.int32)
    d = lane - MLA_NOPE_DIM
    is_rope = (d >= 0) & (d < MLA_ROPE_DIM)
    npair = MLA_ROPE_DIM // 4
    partner = jnp.where(is_rope, jnp.where(d % (2 * npair) < npair, lane + npair, lane - npair), lane)
    partner_cols = lambda w: jnp.where(is_rope, jnp.take(w, partner, axis=-1), 0.0)
    w_in_p = jnp.concatenate([w_cq, w_ckv, w_kr, partner_cols(w_kr)], axis=1).astype(BF16)
    w_uq_p = jnp.pad(w_uq.reshape(MLA_Q_RANK, MLA_HEADS, MLA_QK_DIM), ((0, 0), (0, 0), (0, pad)))
    w_ukv = w_ukv.reshape(MLA_KV_RANK, MLA_HEADS, MLA_NOPE_DIM + MLA_V_DIM)
    w_uk_p = jnp.pad(w_ukv[:, :, :MLA_NOPE_DIM], ((0, 0), (0, 0), (0, LANES - MLA_NOPE_DIM)))
    w_uv = w_ukv[:, :, MLA_NOPE_DIM:]
    slot_gain = lambda g: jnp.pad(g, (0, pad)).reshape(1, LANES)
    flat = lambda w, rows: w.reshape(rows, MLA_SLOTS).astype(BF16)
    return (w_in_p, flat(w_uq_p, MLA_Q_RANK), flat(partner_cols(w_uq_p), MLA_Q_RANK), flat(w_uk_p, MLA_KV_RANK),
            w_uv.reshape(MLA_KV_RANK, MLA_HEADS * MLA_V_DIM).astype(BF16), slot_gain(q_norm), slot_gain(k_norm))


def kernel(x_prompt, x_sample, cache_gqa_k, cache_gqa_v, cache_mla_ckv, cache_mla_krope, c, c_ctx, norm_mix, norm_ffn, w_mod, b_mod, ev_w_in, ev_conv_w, ev_conv_b, ev_filt_w1, ev_filt_b1, ev_filt_w2, ev_filt_b2, ev_filt_w3, ev_filt_b3, ev_filt_freq, ev_hy_bias, ev_q_norm, ev_k_norm, ev_w_out, od_w_in, od_q_lora_norm, od_kv_lora_norm, od_w_uq, od_w_ukv, od_q_norm, od_k_norm, od_w_out, router_w, router_bias, moe_w_gate, moe_w_up, moe_w_down):
    x = jnp.concatenate([_split_parity(x_prompt, BATCH, SEQ), _split_parity(x_sample, DEC_BATCH, DEC_SEQ)], axis=0)
    cond = jnp.concatenate([c_ctx[None, :], c], axis=0)
    cond = jnp.pad(cond, ((0, COND_ROWS - N_COND), (0, 0)))
    mod_all = _modulation(cond, w_mod, b_mod).reshape(DEPTH, COND_ROWS, N_MOD, D_MODEL)

    perm = jnp.arange(N_EXPERTS).reshape(N_GROUPS, EXPERTS_PER_GROUP).T.reshape(-1)
    rw_pm = router_w.T[perm]
    rb_pm = router_bias[perm].reshape(N_EXPERTS, 1)

    dft_ctx, tw_ctx = _conv_constants(SEQ)
    dft_lat, tw_lat = _conv_constants(DEC_SEQ)
    rope_gqa = _rope_tables(0, GQA_HEAD_DIM, GQA_HEAD_DIM)
    rope_mla = _rope_tables(MLA_NOPE_DIM, MLA_ROPE_DIM, LANES)
    ctx_blocks_lat = T_CTX // DEC_SEQ

    st_k, st_v, st_ckv, st_kr = [], [], [], []
    for l in range(DEPTH):
        i = l // 2
        mod = mod_all[l]
        if l % 2 == 0:
            w_in, w_o_hy, w_o_att = _even_weights(ev_w_in[i], ev_w_out[i])
            both = lambda g: jnp.concatenate([g, g]).reshape(1, LANES)
            q_tabs = _fold_gain(rope_gqa, both(ev_q_norm[i]), GQA_HEAD_DIM // 4, GQA_HEAD_DIM ** -0.5 * LOG2E)
            k_tabs = _fold_gain(rope_gqa, both(ev_k_norm[i]), GQA_HEAD_DIM // 4)
            u_hy, q, k_ro, v_bf, k_n, v_32 = _even_in(x, norm_mix[l], mod, w_in, both(ev_k_norm[i]), q_tabs, k_tabs)
            filt_args = (ev_filt_w1[i], ev_filt_b1[i], ev_filt_w2[i], ev_filt_b2[i], ev_filt_w3[i], ev_filt_b3[i],
                         ev_filt_freq[i])
            y_ctx = _hyena(u_hy, 0, SEQ, BATCH, ev_conv_w[i], ev_conv_b[i], dft_ctx, tw_ctx,
                           _filter_tables(SEQ, dft_ctx[0], tw_ctx, filt_args), ev_hy_bias[i])
            y_lat = _hyena(u_hy, ctx_blocks_lat, DEC_SEQ, DEC_BATCH, ev_conv_w[i], ev_conv_b[i], dft_lat, tw_lat,
                           _filter_tables(DEC_SEQ, dft_lat[0], tw_lat, filt_args), ev_hy_bias[i])
            o_ctx = _gqa_attention(q, 0, SEQ, BATCH, SEQ, (k_ro, v_bf))
            cache = (cache_gqa_k[:, i].reshape(DEC_BATCH, PAST_LEN, LANES), cache_gqa_v[:, i].reshape(DEC_BATCH, PAST_LEN, LANES))
            o_lat = _gqa_attention(q, ctx_blocks_lat, DEC_SEQ, DEC_BATCH, 128, (k_ro, v_bf), cache)
            st_k.append(k_n[:T_CTX])
            st_v.append(v_32[:T_CTX])
            x = _out_proj([y_ctx, o_ctx], [y_lat, o_lat], [w_o_hy, w_o_att], x, mod)
        else:
            w_in, w_uq, w_uq_partner, w_uk, w_uv, qg, kg = _odd_weights(od_w_in[i], od_w_uq[i], od_w_ukv[i],
                                                                        od_q_norm[i], od_k_norm[i])
            cos_sin = lambda tabs: (tabs[0], tabs[1] + tabs[2])
            q_tabs = cos_sin(_fold_gain(rope_mla, qg, MLA_ROPE_DIM // 4, MLA_QK_DIM ** -0.5 * LOG2E))
            k_tabs = cos_sin(_fold_gain(rope_mla, kg, MLA_ROPE_DIM // 4))
            q, k, v, ckv, kpe = _odd_in(x, norm_mix[l], mod, w_in, od_q_lora_norm[i].reshape(1, -1),
                                        od_kv_lora_norm[i].reshape(1, -1), w_uq, w_uq_partner, w_uk, w_uv,
                                        q_tabs, k_tabs)
            st_ckv.append(ckv[:T_CTX])
            st_kr.append(kpe[:T_CTX, MLA_NOPE_DIM:MLA_QK_DIM])
            cache_kpe = jnp.pad(cache_mla_krope[:, i].reshape(DEC_BATCH * PAST_LEN, MLA_ROPE_DIM),
                                ((0, 0), (MLA_NOPE_DIM, LANES - MLA_QK_DIM)))
            cache = _mla_cache(cache_mla_ckv[:, i].reshape(DEC_BATCH * PAST_LEN, MLA_KV_RANK), cache_kpe, w_uk, w_uv, kg)
            o_ctx = _mla_attention(q, k, v, 0, SEQ, BATCH, SEQ)
            o_lat = _mla_attention(q, k, v, ctx_blocks_lat, DEC_SEQ, DEC_BATCH, 512, cache)
            x = _out_proj([o_ctx], [o_lat], [od_w_out[i].astype(BF16)], x, mod)

        h, gates_pm = _router(x, norm_ffn[l], mod, rw_pm, rb_pm)
        gates = gates_pm.reshape(EXPERTS_PER_GROUP, N_GROUPS, T_ALL).transpose(2, 1, 0).reshape(T_ALL, N_EXPERTS)
        x = _moe(h, gates, moe_w_gate[l], moe_w_up[l], moe_w_down[l], x, mod)

    y_prompt = _merge_parity(x[:T_CTX], BATCH, SEQ)
    y_sample = _merge_parity(x[T_CTX:], DEC_BATCH, DEC_SEQ)
    state = lambda parts, shape: jnp.stack([_merge_parity(p, BATCH, SEQ) for p in parts], axis=1).reshape(shape)
    return (y_prompt, y_sample,
            state(st_k, (BATCH, DEPTH // 2, SEQ, GQA_KV_HEADS, GQA_HEAD_DIM)),
            state(st_v, (BATCH, DEPTH // 2, SEQ, GQA_KV_HEADS, GQA_HEAD_DIM)),
            state(st_ckv, (BATCH, DEPTH // 2, SEQ, MLA_KV_RANK)),
            state(st_kr, (BATCH, DEPTH // 2, SEQ, MLA_ROPE_DIM)))
```

```python
import functools
import math

import jax
import jax.numpy as jnp
from jax import lax
from jax.experimental import pallas as pl
from jax.experimental.pallas import tpu as pltpu

F32 = jnp.float32
BF16 = jnp.bfloat16

D_MODEL = 1024
BATCH = 16
SEQ = 256
DEPTH = 4
DEC_BATCH = 8
DEC_SEQ = 2048
PAST_LEN = 512
GRID_W = 64
EPS = 1e-6
ROPE_THETA = 10000.0
N_MOD = 6
HY_DIM = D_MODEL // 2
HY_ORDER = 2
FILT_EMB = 33
FILT_BANDS = (FILT_EMB - 1) // 2
HY_MIN_DECAY = math.log(1e-2) / 1.5
HY_MAX_DECAY = math.log(1e-2) / 0.3
GQA_HEADS = 8
GQA_KV_HEADS = 2
GQA_HEAD_DIM = 64
GQA_GROUP = GQA_HEADS // GQA_KV_HEADS
GQA_Q = GQA_HEADS * GQA_HEAD_DIM
GQA_KV = GQA_KV_HEADS * GQA_HEAD_DIM
HY_IN = (HY_ORDER + 1) * HY_DIM
MLA_HEADS = 16
MLA_NOPE_DIM = 64
MLA_ROPE_DIM = 32
MLA_QK_DIM = MLA_NOPE_DIM + MLA_ROPE_DIM
MLA_V_DIM = 64
MLA_Q_RANK = 256
MLA_KV_RANK = 128
N_EXPERTS = 16
N_GROUPS = 4
EXPERTS_PER_GROUP = N_EXPERTS // N_GROUPS
D_EXPERT = 256

LANES = 128
T_CTX = BATCH * SEQ
T_LAT = DEC_BATCH * DEC_SEQ
T_ALL = T_CTX + T_LAT
N_COND = 1 + DEC_BATCH
COND_ROWS = 16
TM = 512
N_CTX_TILES = T_CTX // TM
TILES_PER_SEQ = DEC_SEQ // TM
GQA_SLOTS = GQA_HEADS * LANES
MLA_SLOTS = MLA_HEADS * LANES
EVEN_COLS = HY_IN + GQA_SLOTS + 2 * LANES
HY_CT = 128
KV_CHUNK = 2048
LOG2E = math.log2(math.e)

VMEM_LIMIT_BYTES = 48 * 1024 * 1024


def _params(*sem):
    return pltpu.CompilerParams(dimension_semantics=sem, vmem_limit_bytes=VMEM_LIMIT_BYTES)


def _cond_index(i):
    return (jnp.where(i < N_CTX_TILES, 0, 1 + (i - N_CTX_TILES) // TILES_PER_SEQ), 0, 0)


def _rope_index(i):
    return (jnp.where(i < N_CTX_TILES, 0, 1 + (i - N_CTX_TILES) % TILES_PER_SEQ), 0)


def _ctx_index(i):
    return (jnp.minimum(i, N_CTX_TILES - 1), 0)


def _lat_index(i):
    return (jnp.maximum(i - N_CTX_TILES, 0), 0)


def _const2(i):
    return (0, 0)


def _norm_mod(x, gain, mod_ref, shift_row, scale_row):
    r = lax.rsqrt(jnp.mean(x * x, axis=-1, keepdims=True) + EPS)
    h = (x * r) * gain
    return h * (1.0 + mod_ref[scale_row:scale_row + 1, :]) + mod_ref[shift_row:shift_row + 1, :]


def _slot_norm(t, dims, gain):
    r = lax.rsqrt(jnp.sum(t * t, axis=-1, keepdims=True) / dims + EPS)
    return (t * r) * gain


def _slot_rms(t, dims):
    return lax.rsqrt(jnp.sum(t * t, axis=-1, keepdims=True) / dims + EPS)


def _rope(t, tabs, partner):
    cos, sin_up, sin_dn = tabs
    return t * cos + pltpu.roll(t, partner, 1) * sin_up + pltpu.roll(t, LANES - partner, 1) * sin_dn


def _mod_kernel(c_ref, w_ref, b_ref, o_ref):
    c = c_ref[...]
    a = (c * jax.nn.sigmoid(c)).astype(BF16)
    o_ref[...] = jnp.dot(a, w_ref[...].astype(BF16), preferred_element_type=F32) + b_ref[...]


def _modulation(cond, w_mod, b_mod):
    tn = 1536
    return pl.pallas_call(
        _mod_kernel,
        out_shape=jax.ShapeDtypeStruct((DEPTH, COND_ROWS, N_MOD * D_MODEL), F32),
        grid=(DEPTH, (N_MOD * D_MODEL) // tn),
        in_specs=[
            pl.BlockSpec((COND_ROWS, D_MODEL), lambda l, j: (0, 0)),
            pl.BlockSpec((None, D_MODEL, tn), lambda l, j: (l, 0, j)),
            pl.BlockSpec((None, 1, tn), lambda l, j: (l, 0, j)),
        ],
        out_specs=pl.BlockSpec((None, COND_ROWS, tn), lambda l, j: (l, 0, j)),
        compiler_params=_params("parallel", "parallel"),
        name="modulation",
    )(cond, w_mod, b_mod.reshape(DEPTH, 1, N_MOD * D_MODEL))


def _even_in_kernel(x_ref, g_ref, mod_ref, w_ref, kg_ref, qc_ref, qu_ref, qd_ref, kc_ref, ku_ref, kd_ref,
                    uhy_ref, q_ref, kro_ref, vbf_ref, kn_ref, v32_ref):
    h = _norm_mod(x_ref[...], g_ref[...], mod_ref, 0, 1)
    u = jnp.dot(h.astype(BF16), w_ref[...], preferred_element_type=F32)
    uhy_ref[...] = u[:, :HY_IN].astype(BF16)
    qtabs = (qc_ref[...], qu_ref[...], qd_ref[...])
    partner = GQA_HEAD_DIM // 4
    for s in range(GQA_HEADS):
        qs = u[:, HY_IN + s * LANES:HY_IN + (s + 1) * LANES]
        q_ref[:, s * LANES:(s + 1) * LANES] = (_rope(qs, qtabs, partner) * _slot_rms(qs, GQA_HEAD_DIM)).astype(BF16)
    k = u[:, HY_IN + GQA_SLOTS:HY_IN + GQA_SLOTS + LANES]
    lo = lax.broadcasted_iota(jnp.int32, k.shape, 1) < GQA_HEAD_DIM
    k2 = k * k
    ss_lo = jnp.sum(jnp.where(lo, k2, 0.0), axis=-1, keepdims=True)
    ss_hi = jnp.sum(jnp.where(lo, 0.0, k2), axis=-1, keepdims=True)
    r = jnp.where(lo, lax.rsqrt(ss_lo / GQA_HEAD_DIM + EPS), lax.rsqrt(ss_hi / GQA_HEAD_DIM + EPS))
    kn_ref[...] = (k * r) * kg_ref[...]
    kro_ref[...] = (_rope(k, (kc_ref[...], ku_ref[...], kd_ref[...]), partner) * r).astype(BF16)
    v = u[:, HY_IN + GQA_SLOTS + LANES:]
    v32_ref[...] = v
    vbf_ref[...] = v.astype(BF16)


def _even_in(x, gain, mod, w, kg, qtabs, ktabs):
    tok = lambda n, dt: jax.ShapeDtypeStruct((T_ALL, n), dt)
    row = lambda n: pl.BlockSpec((TM, n), lambda i: (i, 0))
    tab = pl.BlockSpec((TM, LANES), _rope_index)
    return pl.pallas_call(
        _even_in_kernel,
        out_shape=(tok(HY_IN, BF16), tok(GQA_SLOTS, BF16), tok(LANES, BF16), tok(LANES, BF16),
                   tok(LANES, F32), tok(LANES, F32)),
        grid=(T_ALL // TM,),
        in_specs=[row(D_MODEL), pl.BlockSpec((1, D_MODEL), _const2), pl.BlockSpec((None, N_MOD, D_MODEL), _cond_index),
                  pl.BlockSpec((D_MODEL, EVEN_COLS), _const2), pl.BlockSpec((1, LANES), _const2),
                  tab, tab, tab, tab, tab, tab],
        out_specs=(row(HY_IN), row(GQA_SLOTS), row(LANES), row(LANES), row(LANES), row(LANES)),
        compiler_params=_params("parallel"),
        name="even_in",
    )(x, gain.reshape(1, D_MODEL), mod, w, kg, *qtabs, *ktabs)


def _odd_in_kernel(x_ref, g_ref, mod_ref, win_ref, qln_ref, kvln_ref, wuq_ref, wuqp_ref, wuk_ref, wuv_ref,
                   qc_ref, qs_ref, kc_ref, ks_ref, q_ref, k_ref, v_ref, ckv_ref, kpe_ref):
    h = _norm_mod(x_ref[...], g_ref[...], mod_ref, 0, 1)
    u = jnp.dot(h.astype(BF16), win_ref[...], preferred_element_type=F32)
    cq = u[:, :MLA_Q_RANK]
    cq = (cq * lax.rsqrt(jnp.mean(cq * cq, axis=-1, keepdims=True) + EPS)) * qln_ref[...]
    ckv = u[:, MLA_Q_RANK:MLA_Q_RANK + MLA_KV_RANK]
    ckv = (ckv * lax.rsqrt(jnp.mean(ckv * ckv, axis=-1, keepdims=True) + EPS)) * kvln_ref[...]
    kpe = u[:, MLA_Q_RANK + MLA_KV_RANK:MLA_Q_RANK + MLA_KV_RANK + LANES]
    kpe_partner = u[:, MLA_Q_RANK + MLA_KV_RANK + LANES:]
    ckv_ref[...] = ckv
    kpe_ref[...] = kpe
    ckv_b = ckv.astype(BF16)
    cq_b = cq.astype(BF16)
    v_ref[...] = jnp.dot(ckv_b, wuv_ref[...], preferred_element_type=F32).astype(BF16)
    q_all = jnp.dot(cq_b, wuq_ref[...], preferred_element_type=F32)
    q_partner = jnp.dot(cq_b, wuqp_ref[...], preferred_element_type=F32)
    k_all = jnp.dot(ckv_b, wuk_ref[...], preferred_element_type=F32)
    qcos, qsin, kcos, ksin = qc_ref[...], qs_ref[...], kc_ref[...], ks_ref[...]
    k_rot = kpe_partner * ksin
    for s in range(MLA_HEADS):
        sl = slice(s * LANES, (s + 1) * LANES)
        qs = q_all[:, sl]
        q_ref[:, sl] = ((qs * qcos + q_partner[:, sl] * qsin) * _slot_rms(qs, MLA_QK_DIM)).astype(BF16)
        ks = k_all[:, sl] + kpe
        k_ref[:, sl] = ((ks * kcos + k_rot) * _slot_rms(ks, MLA_QK_DIM)).astype(BF16)


def _odd_in(x, gain, mod, win, qln, kvln, wuq, wuqp, wuk, wuv, qtabs, ktabs):
    tok = lambda n, dt: jax.ShapeDtypeStruct((T_ALL, n), dt)
    row = lambda n: pl.BlockSpec((TM, n), lambda i: (i, 0))
    full = lambda a: pl.BlockSpec(a.shape, _const2)
    tab = pl.BlockSpec((TM, LANES), _rope_index)
    return pl.pallas_call(
        _odd_in_kernel,
        out_shape=(tok(MLA_SLOTS, BF16), tok(MLA_SLOTS, BF16), tok(MLA_HEADS * MLA_V_DIM, BF16),
                   tok(LANES, F32), tok(LANES, F32)),
        grid=(T_ALL // TM,),
        in_specs=[row(D_MODEL), pl.BlockSpec((1, D_MODEL), _const2), pl.BlockSpec((None, N_MOD, D_MODEL), _cond_index),
                  full(win), full(qln), full(kvln), full(wuq), full(wuqp), full(wuk), full(wuv), tab, tab, tab, tab],
        out_specs=(row(MLA_SLOTS), row(MLA_SLOTS), row(MLA_HEADS * MLA_V_DIM), row(LANES), row(LANES)),
        compiler_params=_params("parallel"),
        name="odd_in",
    )(x, gain.reshape(1, D_MODEL), mod, win, qln, kvln, wuq, wuqp, wuk, wuv, *qtabs, *ktabs)


def _mla_cache_kernel(ckv_ref, kpe_ref, wuk_ref, wuv_ref, kg_ref, k_ref, v_ref):
    ckv_b = ckv_ref[...].astype(BF16)
    v_ref[...] = jnp.dot(ckv_b, wuv_ref[...], preferred_element_type=F32).astype(BF16)
    k_all = jnp.dot(ckv_b, wuk_ref[...], preferred_element_type=F32)
    kpe, kg = kpe_ref[...], kg_ref[...]
    for s in range(MLA_HEADS):
        sl = slice(s * LANES, (s + 1) * LANES)
        k_ref[:, sl] = _slot_norm(k_all[:, sl] + kpe, MLA_QK_DIM, kg).astype(BF16)


def _mla_cache(ckv, kpe, wuk, wuv, kg):
    rows = ckv.shape[0]
    row = lambda n: pl.BlockSpec((TM, n), lambda i: (i, 0))
    full = lambda a: pl.BlockSpec(a.shape, _const2)
    return pl.pallas_call(
        _mla_cache_kernel,
        out_shape=(jax.ShapeDtypeStruct((rows, MLA_SLOTS), BF16), jax.ShapeDtypeStruct((rows, MLA_HEADS * MLA_V_DIM), BF16)),
        grid=(rows // TM,),
        in_specs=[row(LANES), row(LANES), full(wuk), full(wuv), full(kg)],
        out_specs=(row(MLA_SLOTS), row(MLA_HEADS * MLA_V_DIM)),
        compiler_params=_params("parallel"),
        name="mla_cache",
    )(ckv, kpe, wuk, wuv, kg)


def _out_kernel(*refs, n_parts):
    ctx = refs[:n_parts]
    lat = refs[n_parts:2 * n_parts]
    ws = refs[2 * n_parts:3 * n_parts]
    x_ref, mod_ref, o_ref = refs[3 * n_parts:]
    i = pl.program_id(0)

    def emit(parts):
        y = jnp.dot(parts[0][...], ws[0][...], preferred_element_type=F32)
        for p, w in zip(parts[1:], ws[1:]):
            y += jnp.dot(p[...], w[...], preferred_element_type=F32)
        o_ref[...] = x_ref[...] + mod_ref[2:3, :] * y

    @pl.when(i < N_CTX_TILES)
    def _():
        emit(ctx)

    @pl.when(i >= N_CTX_TILES)
    def _():
        emit(lat)


def _out_proj(parts_ctx, parts_lat, ws, x, mod):
    n_parts = len(ws)
    specs = [pl.BlockSpec((TM, p.shape[1]), _ctx_index) for p in parts_ctx]
    specs += [pl.BlockSpec((TM, p.shape[1]), _lat_index) for p in parts_lat]
    specs += [pl.BlockSpec(w.shape, _const2) for w in ws]
    specs += [pl.BlockSpec((TM, D_MODEL), lambda i: (i, 0)), pl.BlockSpec((None, N_MOD, D_MODEL), _cond_index)]
    return pl.pallas_call(
        functools.partial(_out_kernel, n_parts=n_parts),
        out_shape=jax.ShapeDtypeStruct((T_ALL, D_MODEL), F32),
        grid=(T_ALL // TM,),
        in_specs=specs,
        out_specs=pl.BlockSpec((TM, D_MODEL), lambda i: (i, 0)),
        compiler_params=_params("parallel"),
        name="out_proj",
    )(*parts_ctx, *parts_lat, *ws, x, mod)


def _mm_kernel(a_ref, x_ref, o_ref):
    o_ref[...] = jnp.dot(a_ref[...], x_ref[...], preferred_element_type=F32)


def _matmul(a, x, tn=1024):
    m, k = a.shape
    n = x.shape[1]
    return pl.pallas_call(
        _mm_kernel,
        out_shape=jax.ShapeDtypeStruct((m, n), F32),
        grid=(n // tn,),
        in_specs=[pl.BlockSpec((m, k), _const2), pl.BlockSpec((k, tn), lambda j: (0, j))],
        out_specs=pl.BlockSpec((m, tn), lambda j: (0, j)),
        compiler_params=_params("parallel"),
        name="taps_dft",
    )(a, x)


def _hyena_kernel(v_ref, x1_ref, x2_ref, cw_ref, cb_ref, f_ref, g_ref, tc_ref, ts_ref, h_ref, bias_ref, o_ref):
    n, ct = v_ref.shape
    half = n // 2
    row0 = lax.broadcasted_iota(jnp.int32, (half, ct), 0) == 0
    row_last = lax.broadcasted_iota(jnp.int32, (half, ct), 0) == half - 1
    cos, sin = tc_ref[...], ts_ref[...]

    def conv3(ref, grp):
        s = ref[...].astype(F32)
        se, so = s[:half], s[half:]
        w0, w1, w2 = cw_ref[0, grp:grp + 1, :], cw_ref[1, grp:grp + 1, :], cw_ref[2, grp:grp + 1, :]
        b = cb_ref[grp:grp + 1, :]
        so_prev = jnp.where(row0, 0.0, pltpu.roll(so, 1, 0))
        se_next = jnp.where(row_last, 0.0, pltpu.roll(se, half - 1, 0))
        return (w0 * so_prev + w1 * se + w2 * so + b, w0 * se + w1 * so + w2 * se_next + b)

    def long_conv(ze, zo, order):
        eo = jnp.dot(f_ref[...], jnp.concatenate([ze, zo], axis=1).astype(BF16), preferred_element_type=F32)
        er, ei, orr, oi = eo[:half, :ct], eo[half:, :ct], eo[:half, ct:], eo[half:, ct:]
        har, hai, hbr, hbi = h_ref[order, 0], h_ref[order, 1], h_ref[order, 2], h_ref[order, 3]
        tr = cos * orr + sin * oi
        ti = cos * oi - sin * orr
        pr, pi_, mr, mi = er + tr, ei + ti, er - tr, ei - ti
        ypr = pr * har - pi_ * hai
        ypi = pr * hai + pi_ * har
        ymr = mr * hbr - mi * hbi
        ymi = mr * hbi + mi * hbr
        ar, ai = ypr + ymr, ypi + ymi
        dr, di = ypr - ymr, ypi - ymi
        br = cos * dr - sin * di
        bi = cos * di + sin * dr
        e0, o0, en, on = er[0:1], orr[0:1], ei[0:1], oi[0:1]
        p0 = (e0 + o0) * har[0:1]
        m0 = (e0 - o0) * hbr[0:1]
        ar = jnp.where(row0, p0 + m0, ar)
        br = jnp.where(row0, p0 - m0, br)
        ai = jnp.where(row0, 2.0 * (en * hai[0:1] + on * hbi[0:1]), ai)
        bi = jnp.where(row0, -2.0 * (en * hbi[0:1] - on * hai[0:1]), bi)
        ab = jnp.concatenate([jnp.concatenate([ar, ai], axis=0), jnp.concatenate([br, bi], axis=0)], axis=1)
        y = jnp.dot(g_ref[...], ab.astype(BF16), preferred_element_type=F32)
        bias = bias_ref[order:order + 1, :]
        return y[:, :ct] + bias * ze, y[:, ct:] + bias * zo

    ve, vo = conv3(v_ref, 0)
    x1e, x1o = conv3(x1_ref, 1)
    x2e, x2o = conv3(x2_ref, 2)
    c1e, c1o = long_conv(ve, vo, 0)
    z1e, z1o = x1e * c1e, x1o * c1o
    c2e, c2o = long_conv(z1e, z1o, 1)
    o_ref[:half, :] = (x2e * c2e).astype(o_ref.dtype)
    o_ref[half:, :] = (x2o * c2o).astype(o_ref.dtype)


def _hyena(u_hy, row_block0, n, batch, conv_w, conv_b, dft, twiddle, tables, bias):
    fwd, inv = dft
    half = n // 2
    nj = HY_DIM // HY_CT
    seq = lambda grp: pl.BlockSpec((n, HY_CT), lambda b, j: (row_block0 + b, grp * nj + j))
    return pl.pallas_call(
        _hyena_kernel,
        out_shape=jax.ShapeDtypeStruct((batch * n, HY_DIM), BF16),
        grid=(batch, nj),
        in_specs=[seq(0), seq(1), seq(2),
                  pl.BlockSpec((3, HY_ORDER + 1, HY_CT), lambda b, j: (0, 0, j)),
                  pl.BlockSpec((HY_ORDER + 1, HY_CT), lambda b, j: (0, j)),
                  pl.BlockSpec((n, half), lambda b, j: (0, 0)),
                  pl.BlockSpec((half, n), lambda b, j: (0, 0)),
                  pl.BlockSpec((half, HY_CT), lambda b, j: (0, 0)),
                  pl.BlockSpec((half, HY_CT), lambda b, j: (0, 0)),
                  pl.BlockSpec((HY_ORDER, 4, half, HY_CT), lambda b, j: (0, 0, 0, j)),
                  pl.BlockSpec((HY_ORDER, HY_CT), lambda b, j: (0, j))],
        out_specs=pl.BlockSpec((n, HY_CT), lambda b, j: (b, j)),
        compiler_params=_params("parallel", "parallel"),
        name="hyena",
    )(u_hy, u_hy, u_hy, conv_w.reshape(3, HY_ORDER + 1, HY_DIM), conv_b.reshape(HY_ORDER + 1, HY_DIM),
      fwd, inv, twiddle[0], twiddle[1], tables, bias)


def _attend(q, chunks, vhalf):
    is_value = (lax.broadcasted_iota(jnp.int32, (1, LANES), 1) < LANES // 2) == (vhalf == 0)
    m = acc = None
    for load_k, load_v in chunks:
        s = lax.dot_general(q, load_k(), (((1,), (1,)), ((), ())), preferred_element_type=F32)
        v1 = jnp.where(is_value, load_v(), jnp.ones((), BF16))
        mc = jnp.max(s, axis=-1, keepdims=True)
        if m is None:
            m_new = mc
            acc = jnp.dot(jnp.exp2(s - m_new).astype(BF16), v1, preferred_element_type=F32)
        else:
            m_new = jnp.maximum(m, mc)
            acc = jnp.exp2(m - m_new) * acc + jnp.dot(jnp.exp2(s - m_new).astype(BF16), v1, preferred_element_type=F32)
        m = m_new
    denom = pltpu.roll(acc, LANES // 2, 1)
    return jnp.where(is_value, acc / denom, 0.0)


def _chunks(refs_kv, col):
    out = []
    for k_ref, v_ref in refs_kv:
        for c in range(k_ref.shape[0] // min(KV_CHUNK, k_ref.shape[0])):
            tk = min(KV_CHUNK, k_ref.shape[0])
            out.append((lambda k_ref=k_ref, c=c, tk=tk: k_ref[c * tk:(c + 1) * tk, col * LANES:(col + 1) * LANES].astype(BF16),
                        lambda v_ref=v_ref, c=c, tk=tk: v_ref[c * tk:(c + 1) * tk, :].astype(BF16)))
    return out


def _gqa_kernel(*refs):
    q_ref, o_ref = refs[0], refs[-1]
    kv = [(refs[i], refs[i + 1]) for i in range(1, len(refs) - 1, 2)]
    tq = q_ref.shape[0]
    for g in range(GQA_KV_HEADS):
        q = jnp.concatenate([q_ref[:, (g * GQA_GROUP + j) * LANES:(g * GQA_GROUP + j + 1) * LANES]
                             for j in range(GQA_GROUP)], axis=0)
        o = _attend(q, _chunks(kv, 0), g)
        for j in range(GQA_GROUP):
            s = g * GQA_GROUP + j
            o_ref[:, s * LANES:(s + 1) * LANES] = o[j * tq:(j + 1) * tq].astype(o_ref.dtype)


def _gqa_attention(q, row_block0, n, batch, tq, kv_new, kv_cache=None):
    nq = n // tq
    specs = [pl.BlockSpec((tq, GQA_SLOTS), lambda b, i: (row_block0 * nq + b * nq + i, 0))]
    args = [q]
    if kv_cache is not None:
        specs += [pl.BlockSpec((None, PAST_LEN, LANES), lambda b, i: (b, 0, 0))] * 2
        args += list(kv_cache)
    specs += [pl.BlockSpec((n, LANES), lambda b, i: (row_block0 + b, 0))] * 2
    args += list(kv_new)
    return pl.pallas_call(
        _gqa_kernel,
        out_shape=jax.ShapeDtypeStruct((batch * n, GQA_SLOTS), BF16),
        grid=(batch, nq),
        in_specs=specs,
        out_specs=pl.BlockSpec((tq, GQA_SLOTS), lambda b, i: (b * nq + i, 0)),
        compiler_params=_params("parallel", "parallel"),
        name="gqa_attention",
    )(*args)


def _mla_kernel(*refs):
    q_ref, o_ref = refs[0], refs[-1]
    kv = [(refs[i], refs[i + 1]) for i in range(1, len(refs) - 1, 2)]
    o_ref[...] = (_attend(q_ref[:, :LANES], _chunks(kv, 0), 0)
                  + _attend(q_ref[:, LANES:], _chunks(kv, 1), 1)).astype(o_ref.dtype)


def _mla_attention(q, k_new, v_new, row_block0, n, batch, tq, cache=None):
    nq = n // tq
    pairs = MLA_HEADS // 2
    specs = [pl.BlockSpec((tq, 2 * LANES), lambda b, j, i: (row_block0 * nq + b * nq + i, j))]
    args = [q]
    if cache is not None:
        specs += [pl.BlockSpec((PAST_LEN, 2 * LANES), lambda b, j, i: (b, j)),
                  pl.BlockSpec((PAST_LEN, LANES), lambda b, j, i: (b, j))]
        args += list(cache)
    specs += [pl.BlockSpec((n, 2 * LANES), lambda b, j, i: (row_block0 + b, j)),
              pl.BlockSpec((n, LANES), lambda b, j, i: (row_block0 + b, j))]
    args += [k_new, v_new]
    return pl.pallas_call(
        _mla_kernel,
        out_shape=jax.ShapeDtypeStruct((batch * n, MLA_HEADS * MLA_V_DIM), BF16),
        grid=(batch, pairs, nq),
        in_specs=specs,
        out_specs=pl.BlockSpec((tq, LANES), lambda b, j, i: (b * nq + i, j)),
        compiler_params=_params("parallel", "parallel", "parallel"),
        name="mla_attention",
    )(*args)


def _router_kernel(x_ref, g_ref, mod_ref, rw_ref, rb_ref, h_ref, gate_ref):
    h = _norm_mod(x_ref[...], g_ref[...], mod_ref, 3, 4)
    h_ref[...] = h.astype(BF16)
    logits = lax.dot_general(rw_ref[...], h, (((1,), (1,)), ((), ())),
                             precision=lax.Precision.HIGHEST, preferred_element_type=F32)
    score = jax.nn.sigmoid(logits)
    sel = score + rb_ref[...]
    ng = N_GROUPS
    sv = [sel[p * ng:(p + 1) * ng, :] for p in range(EXPERTS_PER_GROUP)]
    hi01, lo01 = jnp.maximum(sv[0], sv[1]), jnp.minimum(sv[0], sv[1])
    hi23, lo23 = jnp.maximum(sv[2], sv[3]), jnp.minimum(sv[2], sv[3])
    top1 = jnp.maximum(hi01, hi23)
    top2 = jnp.maximum(jnp.minimum(hi01, hi23), jnp.maximum(lo01, lo23))
    gscore = top1 + top2
    gidx = lax.broadcasted_iota(jnp.int32, gscore.shape, 0)
    beaten = jnp.zeros(gscore.shape, jnp.int32)
    for j in range(ng):
        other = gscore[j:j + 1, :]
        beaten += ((other > gscore) | ((other == gscore) & (j < gidx))).astype(jnp.int32)
    best = beaten == 0
    chosen = []
    for p in range(EXPERTS_PER_GROUP):
        rank = jnp.zeros(gscore.shape, jnp.int32)
        for pp in range(EXPERTS_PER_GROUP):
            if pp != p:
                wins = (sv[pp] > sv[p]) | ((sv[pp] == sv[p]) & (pp < p))
                rank += wins.astype(jnp.int32)
        chosen.append((rank < 2) & best)
    picked = [jnp.where(chosen[p], score[p * ng:(p + 1) * ng, :], 0.0) for p in range(EXPERTS_PER_GROUP)]
    total = jnp.sum(picked[0] + picked[1] + picked[2] + picked[3], axis=0, keepdims=True)
    gate_ref[...] = jnp.concatenate(picked, axis=0) / total


def _router(x, gain, mod, rw_pm, rb_pm):
    return pl.pallas_call(
        _router_kernel,
        out_shape=(jax.ShapeDtypeStruct((T_ALL, D_MODEL), BF16), jax.ShapeDtypeStruct((N_EXPERTS, T_ALL), F32)),
        grid=(T_ALL // TM,),
        in_specs=[
            pl.BlockSpec((TM, D_MODEL), lambda i: (i, 0)),
            pl.BlockSpec((1, D_MODEL), _const2),
            pl.BlockSpec((None, N_MOD, D_MODEL), _cond_index),
            pl.BlockSpec((N_EXPERTS, D_MODEL), _const2),
            pl.BlockSpec((N_EXPERTS, 1), _const2),
        ],
        out_specs=(pl.BlockSpec((TM, D_MODEL), lambda i: (i, 0)), pl.BlockSpec((N_EXPERTS, TM), lambda i: (0, i))),
        compiler_params=_params("parallel"),
        name="router",
    )(x, gain.reshape(1, D_MODEL), mod, rw_pm, rb_pm)


MOE_TM = 1024


def _moe_kernel(h_ref, gate_ref, wg_ref, wu_ref, wd_ref, x_ref, mod_ref, o_ref, acc_ref):
    e = pl.program_id(1)

    @pl.when(e == 0)
    def _():
        acc_ref[...] = jnp.zeros_like(acc_ref)

    h = h_ref[...]
    a = jnp.dot(h, wg_ref[...].astype(BF16), preferred_element_type=F32)
    u = jnp.dot(h, wu_ref[...].astype(BF16), preferred_element_type=F32)
    gates = gate_ref[...]
    lane = lax.broadcasted_iota(jnp.int32, gates.shape, 1)
    g = jnp.sum(jnp.where(lane == e, gates, 0.0), axis=1, keepdims=True)
    mid = (a * jax.nn.sigmoid(a)) * u * g
    acc_ref[...] += jnp.dot(mid.astype(BF16), wd_ref[...].astype(BF16), preferred_element_type=F32)

    @pl.when(e == N_EXPERTS - 1)
    def _():
        o_ref[...] = x_ref[...] + mod_ref[5:6, :] * acc_ref[...]


def _moe(h, gates, wg, wu, wd, x, mod):
    tm = MOE_TM
    n_ctx = T_CTX // tm
    per_seq = DEC_SEQ // tm
    cond = lambda i, e: (jnp.where(i < n_ctx, 0, 1 + (i - n_ctx) // per_seq), 0, 0)
    return pl.pallas_call(
        _moe_kernel,
        out_shape=jax.ShapeDtypeStruct((T_ALL, D_MODEL), F32),
        grid=(T_ALL // tm, N_EXPERTS),
        in_specs=[
            pl.BlockSpec((tm, D_MODEL), lambda i, e: (i, 0)),
            pl.BlockSpec((tm, N_EXPERTS), lambda i, e: (i, 0)),
            pl.BlockSpec((None, D_MODEL, D_EXPERT), lambda i, e: (e, 0, 0)),
            pl.BlockSpec((None, D_MODEL, D_EXPERT), lambda i, e: (e, 0, 0)),
            pl.BlockSpec((None, D_EXPERT, D_MODEL), lambda i, e: (e, 0, 0)),
            pl.BlockSpec((tm, D_MODEL), lambda i, e: (i, 0)),
            pl.BlockSpec((None, N_MOD, D_MODEL), cond),
        ],
        out_specs=pl.BlockSpec((tm, D_MODEL), lambda i, e: (i, 0)),
        scratch_shapes=[pltpu.VMEM((tm, D_MODEL), F32)],
        compiler_params=_params("parallel", "arbitrary"),
        name="experts",
    )(h, gates, wg, wu, wd, x, mod)


def _split_parity(a, batch, n):
    rest = a.shape[2:]
    return a.reshape(batch, n // 2, 2, *rest).swapaxes(1, 2).reshape(batch * n, *rest)


def _merge_parity(a, batch, n):
    rest = a.shape[1:]
    return a.reshape(batch, 2, n // 2, *rest).swapaxes(1, 2).reshape(batch, n, *rest)


def _dft_matrices(n):
    f = jnp.arange(n, dtype=jnp.int32)[:, None]
    t = jnp.arange(n, dtype=jnp.int32)[None, :]
    ang = ((f * t) % (2 * n)).astype(F32) * (math.pi / n)
    alt = jnp.where(t % 2 == 0, 1.0, -1.0).astype(F32)
    fwd = jnp.concatenate([jnp.cos(ang), alt, -jnp.sin(ang)[1:]], axis=0)
    row = jnp.arange(2 * n, dtype=jnp.int32)[:, None]
    weight = jnp.where((row == 0) | (row == n), 0.5 / n, 1.0 / n).astype(F32)
    inv = (fwd * weight).T
    return fwd, inv


def _conv_constants(n):
    half = n // 2
    fwd, inv = _dft_matrices(half)
    ang = jnp.arange(half, dtype=F32)[:, None] * (math.pi / n)
    cos = jnp.broadcast_to(jnp.cos(ang), (half, HY_CT))
    sin = jnp.broadcast_to(jnp.sin(ang), (half, HY_CT))
    return (fwd.astype(BF16), (0.5 * inv).astype(BF16)), (cos, sin)


def _filter_taps(n, w1, b1, w2, b2, w3, b3, freq):
    hp = lax.Precision.HIGHEST
    t01 = jnp.linspace(0.0, 1.0, n, dtype=F32)
    w = (2.0 * math.pi / n) * jnp.arange(n, dtype=F32)
    bands = jnp.linspace(1e-4, FILT_BANDS - 1, FILT_BANDS, dtype=F32)
    z = jnp.concatenate([t01[:, None], jnp.cos(w[:, None] * bands[None, :]),
                         -jnp.sin(w[:, None] * bands[None, :])], axis=-1)
    h = jnp.sin(freq * (jnp.dot(z, w1, precision=hp) + b1))
    h = jnp.sin(freq * (jnp.dot(h, w2, precision=hp) + b2))
    h = (jnp.dot(h, w3, precision=hp) + b3).reshape(n, HY_ORDER, 2, HY_DIM)
    deltas = jnp.abs(jnp.linspace(HY_MIN_DECAY, HY_MAX_DECAY, HY_DIM, dtype=F32))
    h = h * jnp.exp(-t01[:, None] * deltas[None, :])[:, None, None, :]
    h_fwd = h[:, :, 0]
    h_bwd = h[:, :, 1] * (jnp.arange(n) > 0)[:, None, None].astype(F32)
    scale = lax.rsqrt(jnp.sum(h_fwd * h_fwd, axis=0, keepdims=True) + jnp.sum(h_bwd * h_bwd, axis=0, keepdims=True) + EPS)
    return h_fwd * scale, h_bwd * scale


def _filter_tables(n, fwd, twiddle, filt_args):
    half = n // 2
    h_fwd, h_bwd = _filter_taps(n, *filt_args)
    cols = HY_ORDER * HY_DIM
    taps = jnp.concatenate([h_fwd.reshape(n, cols), h_bwd.reshape(n, cols)], axis=1)
    eo = _matmul(fwd, jnp.concatenate([taps[0::2], taps[1::2]], axis=1).astype(BF16))
    e, o = eo[:, :2 * cols], eo[:, 2 * cols:]
    er, ei, orr, oi = e[:half], e[half:], o[:half], o[half:]
    cos, sin = twiddle[0][:, :1], twiddle[1][:, :1]
    tr = cos * orr + sin * oi
    ti = cos * oi - sin * orr
    f_, b_ = slice(0, cols), slice(cols, 2 * cols)
    har = (er + tr)[:, f_] + (er + tr)[:, b_]
    hai = (ei + ti)[:, f_] - (ei + ti)[:, b_]
    hbr = (er - tr)[:, f_] + (er - tr)[:, b_]
    hbi = (ei - ti)[:, f_] - (ei - ti)[:, b_]
    first = (jnp.arange(half) == 0)[:, None]
    hai = jnp.where(first, ei[0:1, f_] + ei[0:1, b_], hai)
    hbi = jnp.where(first, oi[0:1, b_] - oi[0:1, f_], hbi)
    tabs = jnp.stack([har, hai, hbr, hbi], axis=0).reshape(4, half, HY_ORDER, HY_DIM)
    return tabs.transpose(2, 0, 1, 3)


def _positions(n):
    r = jnp.arange(n, dtype=jnp.int32)
    return jnp.where(r < n // 2, 2 * r, 2 * (r - n // 2) + 1)


def _rope_tables(first_lane, width, period):
    pos = _positions(DEC_SEQ)
    lane = jnp.arange(LANES, dtype=jnp.int32)
    d = lane % period - first_lane
    active = (d >= 0) & (d < width)
    axis_w = width // 2
    npair = axis_w // 2
    e = d % axis_w
    coord = jnp.where(d < axis_w, (pos // GRID_W)[:, None], (pos % GRID_W)[:, None]).astype(F32)
    freqs = ROPE_THETA ** (-(e % npair).astype(F32) / npair)
    ang = coord * freqs[None, :]
    cos = jnp.where(active[None, :], jnp.cos(ang), 1.0)
    sin = jnp.where(active[None, :], jnp.sin(ang), 0.0)
    second = (e >= npair)[None, :]
    sin_up = jnp.where(second, sin, 0.0)
    sin_dn = jnp.where(second, 0.0, -sin)
    ident = lambda v: jnp.full((TM, LANES), v, F32)
    return (jnp.concatenate([ident(1.0), cos], axis=0), jnp.concatenate([ident(0.0), sin_up], axis=0),
            jnp.concatenate([ident(0.0), sin_dn], axis=0))


def _fold_gain(tabs, gain, partner, scale=1.0):
    cos, sin_up, sin_dn = tabs
    g = gain.reshape(1, LANES) * scale
    return cos * g, sin_up * jnp.roll(g, partner, axis=1), sin_dn * jnp.roll(g, -partner, axis=1)


def _kv_head_of_slot():
    return jax.nn.one_hot(jnp.arange(GQA_HEADS) // GQA_GROUP, GQA_KV_HEADS, dtype=F32)


def _even_weights(w_in, w_out):
    w_hy, w_q, w_k, w_v = jnp.split(w_in, [HY_IN, HY_IN + GQA_Q, HY_IN + GQA_Q + GQA_KV], axis=1)
    sel = _kv_head_of_slot()
    w_q = (w_q.reshape(D_MODEL, GQA_HEADS, 1, GQA_HEAD_DIM) * sel[None, :, :, None]).reshape(D_MODEL, GQA_SLOTS)
    w_in_p = jnp.concatenate([w_hy, w_q, w_k, w_v], axis=1).astype(BF16)
    w_o_hy, w_o_att = w_out[:HY_DIM], w_out[HY_DIM:]
    w_o_att = (w_o_att.reshape(GQA_HEADS, 1, GQA_HEAD_DIM, D_MODEL) * sel[:, :, None, None]).reshape(GQA_SLOTS, D_MODEL)
    return w_in_p, w_o_hy.astype(BF16), w_o_att.astype(BF16)


def _odd_weights(w_in, w_uq, w_ukv, q_norm, k_norm):
    pad = LANES - MLA_QK_DIM
    w_cq, w_ckv, w_kr = jnp.split(w_in, [MLA_Q_RANK, MLA_Q_RANK + MLA_KV_RANK], axis=1)
    w_kr = jnp.pad(w_kr, ((0, 0), (MLA_NOPE_DIM, pad)))
    lane = jnp.arange(LANES, dtype=jnp.int32)
    d = lane - MLA_NOPE_DIM
    is_rope = (d >= 0) & (d < MLA_ROPE_DIM)
    npair = MLA_ROPE_DIM // 4
    partner = jnp.where(is_rope, jnp.where(d % (2 * npair) < npair, lane + npair, lane - npair), lane)
    partner_cols = lambda w: jnp.where(is_rope, jnp.take(w, partner, axis=-1), 0.0)
    w_in_p = jnp.concatenate([w_cq, w_ckv, w_kr, partner_cols(w_kr)], axis=1).astype(BF16)
    w_uq_p = jnp.pad(w_uq.reshape(MLA_Q_RANK, MLA_HEADS, MLA_QK_DIM), ((0, 0), (0, 0), (0, pad)))
    w_ukv = w_ukv.reshape(MLA_KV_RANK, MLA_HEADS, MLA_NOPE_DIM + MLA_V_DIM)
    w_uk_p = jnp.pad(w_ukv[:, :, :MLA_NOPE_DIM], ((0, 0), (0, 0), (0, LANES - MLA_NOPE_DIM)))
    w_uv = w_ukv[:, :, MLA_NOPE_DIM:]
    slot_gain = lambda g: jnp.pad(g, (0, pad)).reshape(1, LANES)
    flat = lambda w, rows: w.reshape(rows, MLA_SLOTS).astype(BF16)
    return (w_in_p, flat(w_uq_p, MLA_Q_RANK), flat(partner_cols(w_uq_p), MLA_Q_RANK), flat(w_uk_p, MLA_KV_RANK),
            w_uv.reshape(MLA_KV_RANK, MLA_HEADS * MLA_V_DIM).astype(BF16), slot_gain(q_norm), slot_gain(k_norm))


def kernel(x_prompt, x_sample, cache_gqa_k, cache_gqa_v, cache_mla_ckv, cache_mla_krope, c, c_ctx, norm_mix, norm_ffn, w_mod, b_mod, ev_w_in, ev_conv_w, ev_conv_b, ev_filt_w1, ev_filt_b1, ev_filt_w2, ev_filt_b2, ev_filt_w3, ev_filt_b3, ev_filt_freq, ev_hy_bias, ev_q_norm, ev_k_norm, ev_w_out, od_w_in, od_q_lora_norm, od_kv_lora_norm, od_w_uq, od_w_ukv, od_q_norm, od_k_norm, od_w_out, router_w, router_bias, moe_w_gate, moe_w_up, moe_w_down):
    x = jnp.concatenate([_split_parity(x_prompt, BATCH, SEQ), _split_parity(x_sample, DEC_BATCH, DEC_SEQ)], axis=0)
    cond = jnp.concatenate([c_ctx[None, :], c], axis=0)
    cond = jnp.pad(cond, ((0, COND_ROWS - N_COND), (0, 0)))
    mod_all = _modulation(cond, w_mod, b_mod).reshape(DEPTH, COND_ROWS, N_MOD, D_MODEL)

    perm = jnp.arange(N_EXPERTS).reshape(N_GROUPS, EXPERTS_PER_GROUP).T.reshape(-1)
    rw_pm = router_w.T[perm]
    rb_pm = router_bias[perm].reshape(N_EXPERTS, 1)

    dft_ctx, tw_ctx = _conv_constants(SEQ)
    dft_lat, tw_lat = _conv_constants(DEC_SEQ)
    rope_gqa = _rope_tables(0, GQA_HEAD_DIM, GQA_HEAD_DIM)
    rope_mla = _rope_tables(MLA_NOPE_DIM, MLA_ROPE_DIM, LANES)
    ctx_blocks_lat = T_CTX // DEC_SEQ

    st_k, st_v, st_ckv, st_kr = [], [], [], []
    for l in range(DEPTH):
        i = l // 2
        mod = mod_all[l]
        if l % 2 == 0:
            w_in, w_o_hy, w_o_att = _even_weights(ev_w_in[i], ev_w_out[i])
            both = lambda g: jnp.concatenate([g, g]).reshape(1, LANES)
            q_tabs = _fold_gain(rope_gqa, both(ev_q_norm[i]), GQA_HEAD_DIM // 4, GQA_HEAD_DIM ** -0.5 * LOG2E)
            k_tabs = _fold_gain(rope_gqa, both(ev_k_norm[i]), GQA_HEAD_DIM // 4)
            u_hy, q, k_ro, v_bf, k_n, v_32 = _even_in(x, norm_mix[l], mod, w_in, both(ev_k_norm[i]), q_tabs, k_tabs)
            filt_args = (ev_filt_w1[i], ev_filt_b1[i], ev_filt_w2[i], ev_filt_b2[i], ev_filt_w3[i], ev_filt_b3[i],
                         ev_filt_freq[i])
            y_ctx = _hyena(u_hy, 0, SEQ, BATCH, ev_conv_w[i], ev_conv_b[i], dft_ctx, tw_ctx,
                           _filter_tables(SEQ, dft_ctx[0], tw_ctx, filt_args), ev_hy_bias[i])
            y_lat = _hyena(u_hy, ctx_blocks_lat, DEC_SEQ, DEC_BATCH, ev_conv_w[i], ev_conv_b[i], dft_lat, tw_lat,
                           _filter_tables(DEC_SEQ, dft_lat[0], tw_lat, filt_args), ev_hy_bias[i])
            o_ctx = _gqa_attention(q, 0, SEQ, BATCH, SEQ, (k_ro, v_bf))
            cache = (cache_gqa_k[:, i].reshape(DEC_BATCH, PAST_LEN, LANES), cache_gqa_v[:, i].reshape(DEC_BATCH, PAST_LEN, LANES))
            o_lat = _gqa_attention(q, ctx_blocks_lat, DEC_SEQ, DEC_BATCH, 256, (k_ro, v_bf), cache)
            st_k.append(k_n[:T_CTX])
            st_v.append(v_32[:T_CTX])
            x = _out_proj([y_ctx, o_ctx], [y_lat, o_lat], [w_o_hy, w_o_att], x, mod)
        else:
            w_in, w_uq, w_uq_partner, w_uk, w_uv, qg, kg = _odd_weights(od_w_in[i], od_w_uq[i], od_w_ukv[i],
                                                                        od_q_norm[i], od_k_norm[i])
            cos_sin = lambda tabs: (tabs[0], tabs[1] + tabs[2])
            q_tabs = cos_sin(_fold_gain(rope_mla, qg, MLA_ROPE_DIM // 4, MLA_QK_DIM ** -0.5 * LOG2E))
            k_tabs = cos_sin(_fold_gain(rope_mla, kg, MLA_ROPE_DIM // 4))
            q, k, v, ckv, kpe = _odd_in(x, norm_mix[l], mod, w_in, od_q_lora_norm[i].reshape(1, -1),
                                        od_kv_lora_norm[i].reshape(1, -1), w_uq, w_uq_partner, w_uk, w_uv,
                                        q_tabs, k_tabs)
            st_ckv.append(ckv[:T_CTX])
            st_kr.append(kpe[:T_CTX, MLA_NOPE_DIM:MLA_QK_DIM])
            cache_kpe = jnp.pad(cache_mla_krope[:, i].reshape(DEC_BATCH * PAST_LEN, MLA_ROPE_DIM),
                                ((0, 0), (MLA_NOPE_DIM, LANES - MLA_QK_DIM)))
            cache = _mla_cache(cache_mla_ckv[:, i].reshape(DEC_BATCH * PAST_LEN, MLA_KV_RANK), cache_kpe, w_uk, w_uv, kg)
            o_ctx = _mla_attention(q, k, v, 0, SEQ, BATCH, SEQ)
            o_lat = _mla_attention(q, k, v, ctx_blocks_lat, DEC_SEQ, DEC_BATCH, 1024, cache)
            x = _out_proj([o_ctx], [o_lat], [od_w_out[i].astype(BF16)], x, mod)

        h, gates_pm = _router(x, norm_ffn[l], mod, rw_pm, rb_pm)
        gates = gates_pm.reshape(EXPERTS_PER_GROUP, N_GROUPS, T_ALL).transpose(2, 1, 0).reshape(T_ALL, N_EXPERTS)
        x = _moe(h, gates, moe_w_gate[l], moe_w_up[l], moe_w_down[l], x, mod)

    y_prompt = _merge_parity(x[:T_CTX], BATCH, SEQ)
    y_sample = _merge_parity(x[T_CTX:], DEC_BATCH, DEC_SEQ)
    state = lambda parts, shape: jnp.stack([_merge_parity(p, BATCH, SEQ) for p in parts], axis=1).reshape(shape)
    return (y_prompt, y_sample,
            state(st_k, (BATCH, DEPTH // 2, SEQ, GQA_KV_HEADS, GQA_HEAD_DIM)),
            state(st_v, (BATCH, DEPTH // 2, SEQ, GQA_KV_HEADS, GQA_HEAD_DIM)),
            state(st_ckv, (BATCH, DEPTH // 2, SEQ, MLA_KV_RANK)),
            state(st_kr, (BATCH, DEPTH // 2, SEQ, MLA_ROPE_DIM)))
```
